```python
import math
import jax, jax.numpy as jnp
from jax import lax
import numpy as np

D_MODEL = 1024
BATCH = 8
SEQ = 2048
DEPTH = 4
DEC_BATCH = 128
DEC_SEQ = 4
PAST_LEN = 8192
PAGE_SIZE = 128

N_MIXERS = 2
N_DELTA_LAYERS = (DEPTH + 1) // 2
N_SWA_LAYERS = DEPTH // 2

DN_QK_HEADS = 8
DN_V_HEADS = 16
DN_V_PER_QK = DN_V_HEADS // DN_QK_HEADS
DN_HEAD_K = 128
DN_HEAD_V = 128
DN_KEY_DIM = DN_QK_HEADS * DN_HEAD_K
DN_VAL_DIM = DN_V_HEADS * DN_HEAD_V
DN_CONV_DIM = 2 * DN_KEY_DIM + DN_VAL_DIM
DN_CONV = 4
DN_CHUNK = 64
DN_IN_DIM = DN_CONV_DIM + DN_VAL_DIM + 2 * DN_V_HEADS

SWA_HEADS = 16
SWA_KV_HEADS = 4
SWA_GROUPS = SWA_HEADS // SWA_KV_HEADS
SWA_HEAD_DIM = 64
SWA_QKV_DIM = (SWA_HEADS + 2 * SWA_KV_HEADS) * SWA_HEAD_DIM
SWA_SCALE = SWA_HEAD_DIM ** -0.5
WINDOW = 128

D_FF = -(-8 * D_MODEL // (3 * 256)) * 256

RMS_EPS = 1e-6
L2_EPS = 1e-6

kernel_name = "hybrid_gdn_swa_sink_alibi_step"


def rmsnorm(x, w):
    xf = x.astype(jnp.float32)
    y = xf * lax.rsqrt(jnp.mean(xf * xf, -1, keepdims=True) + RMS_EPS)
    return (y * w.astype(jnp.float32)).astype(x.dtype)


def l2norm(x):
    return x * lax.rsqrt(jnp.sum(x * x, -1, keepdims=True) + L2_EPS)


def swiglu(h, w_gu, w_down):
    gu = h @ w_gu
    return (jax.nn.silu(gu[..., :D_FF]) * gu[..., D_FF:]) @ w_down


def short_conv(u, buf, w):
    L = u.shape[1]
    full = jnp.concatenate([buf.astype(u.dtype), u], 1)
    y = sum(full[:, k:k + L] * w[k] for k in range(DN_CONV))
    return jax.nn.silu(y), full[:, L:]


def gated_delta_chunked(q, k, v, g, beta, s0):
    B, L, H, DK = q.shape
    C = min(DN_CHUNK, L)
    pad = (-L) % C
    if pad:
        pw = ((0, 0), (0, pad), (0, 0), (0, 0))
        q, k, v = jnp.pad(q, pw), jnp.pad(k, pw), jnp.pad(v, pw)
        g, beta = jnp.pad(g, pw[:3]), jnp.pad(beta, pw[:3])
    N = (L + pad) // C

    def blk(t):
        return jnp.moveaxis(t.reshape((B, N, C) + t.shape[2:]), 3, 1)

    q, k, v, g, beta = blk(q), blk(k), blk(v), blk(g), blk(beta)
    G = jnp.cumsum(g, -1)
    idx = jnp.arange(C)
    causal = idx[:, None] >= idx[None, :]
    strict = idx[:, None] > idx[None, :]
    diff = G[..., :, None] - G[..., None, :]
    decay = jnp.where(causal, jnp.exp(jnp.where(causal, diff, 0.0)), 0.0)
    kb = k * beta[..., None]
    Bm = jnp.where(strict, jnp.einsum('bhncd,bhnsd->bhncs', kb, k) * decay, 0.0)
    M = -Bm
    T = jnp.eye(C, dtype=jnp.float32) + M
    Mp = M
    for _ in range(1, max(1, (C - 1).bit_length())):
        Mp = Mp @ Mp
        T = T + T @ Mp
    u = jnp.einsum('bhncs,bhnse->bhnce', T, v * beta[..., None])
    w = jnp.einsum('bhncs,bhnsd->bhncd', T, kb * jnp.exp(G)[..., None])
    aqk = jnp.einsum('bhncd,bhnsd->bhncs', q, k) * decay

    def step(S, xs):
        qc, kc, uc, wc, Gc, ac = xs
        unew = uc - jnp.einsum('bhcd,bhde->bhce', wc, S)
        o = (jnp.einsum('bhcd,bhde->bhce', qc * jnp.exp(Gc)[..., None], S)
             + jnp.einsum('bhcs,bhse->bhce', ac, unew))
        Gl = Gc[..., -1:]
        S = S * jnp.exp(Gl)[..., None] + jnp.einsum('bhcd,bhce->bhde', kc * jnp.exp(Gl - Gc)[..., None], unew)
        return S, o

    xs = tuple(jnp.moveaxis(t, 2, 0) for t in (q, k, u, w, G, aqk))
    s_new, o = lax.scan(step, s0, xs)
    o = jnp.transpose(o, (1, 0, 3, 2, 4)).reshape(B, N * C, H, -1)[:, :L]
    return o, s_new


def deltanet_mixer(h, conv_buf, s0, w_in, conv_w, a_log, dt_bias, norm_w, w_out):
    B, L, _ = h.shape
    f32 = jnp.float32
    proj = h @ w_in
    o1 = DN_CONV_DIM
    o2 = o1 + DN_VAL_DIM
    o3 = o2 + DN_V_HEADS
    qkv, new_buf = short_conv(proj[..., :o1], conv_buf, conv_w)
    z, b, a = proj[..., o1:o2], proj[..., o2:o3], proj[..., o3:]
    qkv = qkv.astype(f32)
    q = qkv[..., :DN_KEY_DIM].reshape(B, L, DN_QK_HEADS, DN_HEAD_K)
    k = qkv[..., DN_KEY_DIM:2 * DN_KEY_DIM].reshape(B, L, DN_QK_HEADS, DN_HEAD_K)
    v = qkv[..., 2 * DN_KEY_DIM:].reshape(B, L, DN_V_HEADS, DN_HEAD_V)
    q = jnp.repeat(l2norm(q) * DN_HEAD_K ** -0.5, DN_V_PER_QK, axis=2)
    k = jnp.repeat(l2norm(k), DN_V_PER_QK, axis=2)
    beta = jax.nn.sigmoid(b.astype(f32))
    g = -jnp.exp(a_log.astype(f32)) * jax.nn.softplus(a.astype(f32) + dt_bias.astype(f32))
    o, s_new = gated_delta_chunked(q, k, v, g, beta, s0.astype(f32))
    o = o * lax.rsqrt(jnp.mean(o * o, -1, keepdims=True) + RMS_EPS) * norm_w.astype(f32)
    o = o * jax.nn.silu(z.astype(f32).reshape(B, L, DN_V_HEADS, DN_HEAD_V))
    out = o.reshape(B, L, DN_VAL_DIM).astype(h.dtype) @ w_out
    return out, new_buf, s_new


def alibi_slopes():
    s = 2.0 ** (-8.0 * np.arange(1, SWA_HEADS + 1) / SWA_HEADS)
    return jnp.asarray(s, dtype=jnp.float32).reshape(SWA_KV_HEADS, SWA_GROUPS, 1, 1)


def swa_project(h, w_qkv, b_qkv):
    B, L, _ = h.shape
    qkv = h @ w_qkv + b_qkv
    nq = SWA_HEADS * SWA_HEAD_DIM
    nk = SWA_KV_HEADS * SWA_HEAD_DIM
    q = qkv[..., :nq].reshape(B, L, SWA_KV_HEADS, SWA_GROUPS, SWA_HEAD_DIM)
    k = qkv[..., nq:nq + nk].reshape(B, L, SWA_KV_HEADS, SWA_HEAD_DIM)
    v = qkv[..., nq + nk:].reshape(B, L, SWA_KV_HEADS, SWA_HEAD_DIM)
    return q, k, v


def sink_probs(s, rel, mask, sinks):
    s = s.astype(jnp.float32) * SWA_SCALE - alibi_slopes() * rel.astype(jnp.float32)
    s = jnp.where(mask, s, -jnp.inf)
    sk = sinks.astype(jnp.float32).reshape(SWA_KV_HEADS, SWA_GROUPS, 1, 1)
    m = jnp.maximum(jnp.max(s, -1, keepdims=True), sk)
    p = jnp.exp(s - m)
    return p / (jnp.sum(p, -1, keepdims=True) + jnp.exp(sk - m))


def swa_prompt(h, w_qkv, b_qkv, sinks, w_o, b_o):
    B, L, _ = h.shape
    q, k, v = swa_project(h, w_qkv, b_qkv)
    W = WINDOW
    pad = (-L) % W
    qp, kp, vp = q, k, v
    if pad:
        qp = jnp.pad(q, ((0, 0), (0, pad), (0, 0), (0, 0), (0, 0)))
        kp = jnp.pad(k, ((0, 0), (0, pad), (0, 0), (0, 0)))
        vp = jnp.pad(v, ((0, 0), (0, pad), (0, 0), (0, 0)))
    nb = (L + pad) // W
    qb = qp.reshape(B, nb, W, SWA_KV_HEADS, SWA_GROUPS, SWA_HEAD_DIM)
    kb = kp.reshape(B, nb, W, SWA_KV_HEADS, SWA_HEAD_DIM)
    vb = vp.reshape(B, nb, W, SWA_KV_HEADS, SWA_HEAD_DIM)
    prev = lambda t: jnp.concatenate([jnp.zeros_like(t[:, :1]), t[:, :-1]], 1)
    kk = jnp.concatenate([prev(kb), kb], 2)
    vv = jnp.concatenate([prev(vb), vb], 2)
    s = jnp.einsum('bnqkgd,bnskd->bnkgqs', qb, kk)
    i = jnp.arange(W)[:, None]
    j = jnp.arange(2 * W)[None, :]
    rel = W + i - j
    band = (rel >= 0) & (rel <= WINDOW)
    blkid = jnp.arange(nb)[:, None, None]
    mask = band[None] & ((blkid > 0) | (j[None] >= W))
    p = sink_probs(s, rel, mask[None, :, None, None], sinks).astype(vv.dtype)
    o = jnp.einsum('bnkgqs,bnskd->bnqkgd', p, vv).reshape(B, nb * W, SWA_HEADS * SWA_HEAD_DIM)[:, :L]
    R = min(WINDOW, L)
    return o @ w_o + b_o, k[:, L - R:], v[:, L - R:]


def swa_sample(h, cache_k, cache_v, w_qkv, b_qkv, sinks, w_o, b_o):
    B, L, _ = h.shape
    q, k, v = swa_project(h, w_qkv, b_qkv)
    R = cache_k.shape[1]
    kk = jnp.concatenate([cache_k.astype(k.dtype), k], 1)
    vv = jnp.concatenate([cache_v.astype(v.dtype), v], 1)
    s = jnp.einsum('bqkgd,bskd->bkgqs', q, kk)
    rel = (R + jnp.arange(L))[:, None] - jnp.arange(R + L)[None, :]
    mask = (rel >= 0) & (rel <= WINDOW)
    p = sink_probs(s, rel, mask, sinks).astype(vv.dtype)
    o = jnp.einsum('bkgqs,bskd->bqkgd', p, vv).reshape(B, L, SWA_HEADS * SWA_HEAD_DIM)
    return o @ w_o + b_o, k, v


def setup_inputs(seed: int = 0) -> dict:
    key = jax.random.key(seed)
    ks = jax.random.split(key, 26)
    f32 = jnp.float32
    nrm = lambda k, shape, scale: jax.random.normal(k, shape, f32) * scale
    NA, NB = N_DELTA_LAYERS, N_SWA_LAYERS
    cache_rows = min(WINDOW, PAST_LEN)
    dt = jnp.exp(jax.random.uniform(ks[13], (NA, DN_V_HEADS), f32, math.log(1e-3), math.log(1e-1)))
    return {
        "x_prompt": nrm(ks[0], (BATCH, SEQ, D_MODEL), 1.0),
        "x_sample": nrm(ks[1], (DEC_BATCH, DEC_SEQ, D_MODEL), 1.0),
        "state_conv": nrm(ks[2], (NA, DEC_BATCH, DN_CONV - 1, DN_CONV_DIM), 1.0),
        "state_delta": nrm(ks[3], (NA, DEC_BATCH, DN_V_HEADS, DN_HEAD_K, DN_HEAD_V), 0.5),
        "cache_k": nrm(ks[4], (NB, DEC_BATCH, cache_rows, SWA_KV_HEADS, SWA_HEAD_DIM), 1.0),
        "cache_v": nrm(ks[5], (NB, DEC_BATCH, cache_rows, SWA_KV_HEADS, SWA_HEAD_DIM), 1.0),
        "norm_mix": 1.0 + nrm(ks[6], (DEPTH, D_MODEL), 0.05),
        "norm_ffn": 1.0 + nrm(ks[7], (DEPTH, D_MODEL), 0.05),
        "norm_final": 1.0 + nrm(ks[8], (D_MODEL,), 0.05),
        "dn_w_in": nrm(ks[9], (NA, D_MODEL, DN_IN_DIM), D_MODEL ** -0.5),
        "dn_conv_w": nrm(ks[10], (NA, DN_CONV, DN_CONV_DIM), DN_CONV ** -0.5),
        "dn_a_log": jnp.log(jax.random.uniform(ks[11], (NA, DN_V_HEADS), f32, 1.0, 16.0)),
        "dn_dt_bias": dt + jnp.log(-jnp.expm1(-dt)),
        "dn_norm_w": 1.0 + nrm(ks[12], (NA, DN_HEAD_V), 0.05),
        "dn_w_out": nrm(ks[14], (NA, DN_VAL_DIM, D_MODEL), DN_VAL_DIM ** -0.5),
        "swa_w_qkv": nrm(ks[15], (NB, D_MODEL, SWA_QKV_DIM), D_MODEL ** -0.5),
        "swa_b_qkv": nrm(ks[16], (NB, SWA_QKV_DIM), 0.02),
        "swa_sinks": nrm(ks[17], (NB, SWA_HEADS), 1.0),
        "swa_w_o": nrm(ks[18], (NB, SWA_HEADS * SWA_HEAD_DIM, D_MODEL), (SWA_HEADS * SWA_HEAD_DIM) ** -0.5),
        "swa_b_o": nrm(ks[19], (NB, D_MODEL), 0.02),
        "ffn_w_gu": nrm(ks[20], (DEPTH, D_MODEL, 2 * D_FF), D_MODEL ** -0.5),
        "ffn_w_down": nrm(ks[21], (DEPTH, D_FF, D_MODEL), D_FF ** -0.5),
    }


def reference(x_prompt, x_sample, state_conv, state_delta, cache_k, cache_v,
              norm_mix, norm_ffn, norm_final,
              dn_w_in, dn_conv_w, dn_a_log, dn_dt_bias, dn_norm_w, dn_w_out,
              swa_w_qkv, swa_b_qkv, swa_sinks, swa_w_o, swa_b_o,
              ffn_w_gu, ffn_w_down):
    yp, ys = x_prompt, x_sample
    conv_p, delta_p, kp_list, vp_list = [], [], [], []
    conv_s, delta_s, ks_list, vs_list = [], [], [], []
    Bp = x_prompt.shape[0]
    for i in range(DEPTH):
        j = i // N_MIXERS
        hp = rmsnorm(yp, norm_mix[i])
        hs = rmsnorm(ys, norm_mix[i])
        if i % N_MIXERS == 0:
            wts = (dn_w_in[j], dn_conv_w[j], dn_a_log[j], dn_dt_bias[j], dn_norm_w[j], dn_w_out[j])
            buf0 = jnp.zeros((Bp, DN_CONV - 1, DN_CONV_DIM), hp.dtype)
            s00 = jnp.zeros((Bp, DN_V_HEADS, DN_HEAD_K, DN_HEAD_V), jnp.float32)
            op, cbp, sp = deltanet_mixer(hp, buf0, s00, *wts)
            os_, cbs, ss = deltanet_mixer(hs, state_conv[j], state_delta[j], *wts)
            conv_p.append(cbp.astype(state_conv.dtype))
            delta_p.append(sp.astype(state_delta.dtype))
            conv_s.append(cbs.astype(state_conv.dtype))
            delta_s.append(ss.astype(state_delta.dtype))
        else:
            wts = (swa_w_qkv[j], swa_b_qkv[j], swa_sinks[j], swa_w_o[j], swa_b_o[j])
            op, kpn, vpn = swa_prompt(hp, *wts)
            os_, ksn, vsn = swa_sample(hs, cache_k[j], cache_v[j], *wts)
            kp_list.append(kpn.astype(cache_k.dtype))
            vp_list.append(vpn.astype(cache_v.dtype))
            ks_list.append(ksn.astype(cache_k.dtype))
            vs_list.append(vsn.astype(cache_v.dtype))
        yp = yp + op
        ys = ys + os_
        yp = yp + swiglu(rmsnorm(yp, norm_ffn[i]), ffn_w_gu[i], ffn_w_down[i])
        ys = ys + swiglu(rmsnorm(ys, norm_ffn[i]), ffn_w_gu[i], ffn_w_down[i])
    y_prompt = rmsnorm(yp, norm_final)
    y_sample = rmsnorm(ys, norm_final)
    return (y_prompt, y_sample,
            jnp.stack(conv_p), jnp.stack(delta_p), jnp.stack(kp_list), jnp.stack(vp_list),
            jnp.stack(conv_s), jnp.stack(delta_s), jnp.stack(ks_list), jnp.stack(vs_list))
```

```python
import functools
import math

import numpy as np
import jax
import jax.numpy as jnp
from jax import lax
from jax.experimental import pallas as pl
from jax.experimental.pallas import tpu as pltpu

f32 = jnp.float32
bf16 = jnp.bfloat16

D_MODEL = 1024
DN_QK_HEADS = 8
DN_V_HEADS = 16
DN_HEAD = 128
DN_KEY_DIM = DN_QK_HEADS * DN_HEAD
DN_VAL_DIM = DN_V_HEADS * DN_HEAD
DN_CONV_DIM = 2 * DN_KEY_DIM + DN_VAL_DIM
DN_CONV = 4
DN_IN_DIM = DN_CONV_DIM + DN_VAL_DIM + 2 * DN_V_HEADS
DN_IN_PAD = 6272
SWA_HEADS = 16
SWA_KV_HEADS = 4
SWA_GROUPS = SWA_HEADS // SWA_KV_HEADS
SWA_HEAD_DIM = 64
SWA_Q_DIM = SWA_HEADS * SWA_HEAD_DIM
SWA_KV_DIM = SWA_KV_HEADS * SWA_HEAD_DIM
SWA_QKV_DIM = SWA_Q_DIM + 2 * SWA_KV_DIM
SWA_SCALE = SWA_HEAD_DIM ** -0.5
WINDOW = 128
D_FF = 2816
RMS_EPS = 1e-6
L2_EPS = 1e-6
ALIBI_SLOPES = tuple(float(2.0 ** (-8.0 * (i + 1) / SWA_HEADS)) for i in range(SWA_HEADS))

VMEM_LIMIT_BYTES = 56 * 1024 * 1024
DN_CHUNK = 64


def _params(*sem):
    return pltpu.CompilerParams(dimension_semantics=sem, vmem_limit_bytes=VMEM_LIMIT_BYTES)


def _sigmoid(x):
    return 1.0 / (1.0 + jnp.exp(-x))


def _rms(x, w):
    return x * lax.rsqrt(jnp.mean(x * x, axis=-1, keepdims=True) + RMS_EPS) * w


def _dot(a, b):
    return jnp.dot(a, b, preferred_element_type=f32)


def _dot_nt(a, b):
    return lax.dot_general(a, b, (((1,), (1,)), ((), ())), preferred_element_type=f32)


def _rms_matmul_kernel(x_ref, nw_ref, w_ref, b_ref, o_ref, xn_ref):
    @pl.when(pl.program_id(1) == 0)
    def _():
        xn_ref[...] = _rms(x_ref[...], nw_ref[...]).astype(bf16)

    o_ref[...] = _dot(xn_ref[...], w_ref[...]) + b_ref[...]


def rms_matmul(x, nw, w, b, tn):
    m, d = x.shape
    n = w.shape[1]
    tm = min(m, 1024)
    return pl.pallas_call(
        _rms_matmul_kernel,
        grid=(m // tm, n // tn),
        in_specs=[
            pl.BlockSpec((tm, d), lambda i, j: (i, 0)),
            pl.BlockSpec((1, d), lambda i, j: (0, 0)),
            pl.BlockSpec((d, tn), lambda i, j: (0, j)),
            pl.BlockSpec((1, tn), lambda i, j: (0, j)),
        ],
        out_specs=pl.BlockSpec((tm, tn), lambda i, j: (i, j)),
        out_shape=jax.ShapeDtypeStruct((m, n), f32),
        scratch_shapes=[pltpu.VMEM((tm, d), bf16)],
        compiler_params=_params("parallel", "arbitrary"),
        name="rms_matmul",
    )(x, nw.reshape(1, d), w, b.reshape(1, n))


def _matmul_resid_kernel(a_ref, w_ref, b_ref, r_ref, o_ref):
    o_ref[...] = r_ref[...] + _dot(a_ref[...], w_ref[...]) + b_ref[...]


def matmul_resid(a, w, b, resid):
    m, k = a.shape
    n = w.shape[1]
    tm = min(m, 512)
    return pl.pallas_call(
        _matmul_resid_kernel,
        grid=(m // tm,),
        in_specs=[
            pl.BlockSpec((tm, k), lambda i: (i, 0)),
            pl.BlockSpec((k, n), lambda i: (0, 0)),
            pl.BlockSpec((1, n), lambda i: (0, 0)),
            pl.BlockSpec((tm, n), lambda i: (i, 0)),
        ],
        out_specs=pl.BlockSpec((tm, n), lambda i: (i, 0)),
        out_shape=jax.ShapeDtypeStruct((m, n), f32),
        compiler_params=_params("parallel"),
        name="matmul_resid",
    )(a, w, b.reshape(1, n), resid)


def _ffn_kernel(final_norm, x_ref, nw_ref, wg_ref, wu_ref, wd_ref, nf_ref, o_ref, xn_ref, acc_ref):
    j = pl.program_id(1)

    @pl.when(j == 0)
    def _():
        xn_ref[...] = _rms(x_ref[...], nw_ref[...]).astype(bf16)
        acc_ref[...] = jnp.zeros_like(acc_ref)

    xn = xn_ref[...]
    g = _dot(xn, wg_ref[...])
    u = _dot(xn, wu_ref[...])
    h = (g * _sigmoid(g) * u).astype(bf16)
    acc_ref[...] += _dot(h, wd_ref[...])

    @pl.when(j == pl.num_programs(1) - 1)
    def _():
        y = x_ref[...] + acc_ref[...]
        if final_norm:
            y = _rms(y, nf_ref[...])
        o_ref[...] = y


def ffn(x, nw, w_gu, w_down, nf, final_norm):
    m, d = x.shape
    tm = min(m, 512)
    tf = D_FF // 2
    nj = D_FF // tf
    return pl.pallas_call(
        functools.partial(_ffn_kernel, final_norm),
        grid=(m // tm, nj),
        in_specs=[
            pl.BlockSpec((tm, d), lambda i, j: (i, 0)),
            pl.BlockSpec((1, d), lambda i, j: (0, 0)),
            pl.BlockSpec((d, tf), lambda i, j: (0, j)),
            pl.BlockSpec((d, tf), lambda i, j: (0, j + nj)),
            pl.BlockSpec((tf, d), lambda i, j: (j, 0)),
            pl.BlockSpec((1, d), lambda i, j: (0, 0)),
        ],
        out_specs=pl.BlockSpec((tm, d), lambda i, j: (i, 0)),
        out_shape=jax.ShapeDtypeStruct((m, d), f32),
        scratch_shapes=[pltpu.VMEM((tm, d), bf16), pltpu.VMEM((tm, d), f32)],
        compiler_params=_params("parallel", "arbitrary"),
        name="ffn",
    )(x, nw.reshape(1, d), w_gu, w_gu, w_down, nf.reshape(1, d))


def _delta_kernel(C, n_iter, proj_ref, conv0_ref, s0_ref, cw_ref, alog_ref, dtb_ref, nw_ref,
                  o_ref, convout_ref, sout_ref, full_ref, qn_ref, kn_ref, v_ref):
    n = pl.program_id(1)
    H = DN_HEAD

    @pl.when(n == 0)
    def _():
        full_ref[5:8, :] = conv0_ref[0]
        sout_ref[...] = s0_ref[...]

    full_ref[8:8 + C, :] = proj_ref[0, :, 0:DN_CONV_DIM]

    for cb in range(DN_CONV_DIM // H):
        cols = slice(cb * H, (cb + 1) * H)
        w = cw_ref[:, cols]
        y = (full_ref[5:5 + C, cols] * w[0:1] + full_ref[6:6 + C, cols] * w[1:2]
             + full_ref[7:7 + C, cols] * w[2:3] + full_ref[8:8 + C, cols] * w[3:4])
        y = y * _sigmoid(y)
        if cb < 2 * DN_QK_HEADS:
            inv = lax.rsqrt(jnp.sum(y * y, axis=-1, keepdims=True) + L2_EPS)
            if cb < DN_QK_HEADS:
                qn_ref[cb] = y * (inv * (H ** -0.5))
            else:
                kn_ref[cb - DN_QK_HEADS] = y * inv
        else:
            v_ref[cb - 2 * DN_QK_HEADS] = y

    tail = full_ref[C + 5:C + 8, :]
    full_ref[5:8, :] = tail

    @pl.when(n == pl.num_programs(1) - 1)
    def _():
        convout_ref[0] = tail

    ba = proj_ref[0, :, DN_CONV_DIM + DN_VAL_DIM:DN_IN_PAD]
    beta = _sigmoid(ba[:, 0:DN_V_HEADS])
    xa = ba[:, DN_V_HEADS:2 * DN_V_HEADS] + dtb_ref[...]
    softplus = jnp.maximum(xa, 0.0) + jnp.log(1.0 + jnp.exp(-jnp.abs(xa)))
    g = -jnp.exp(alog_ref[...]) * softplus

    ri = lax.broadcasted_iota(jnp.int32, (C, C), 0)
    ci = lax.broadcasted_iota(jnp.int32, (C, C), 1)
    causal = ri >= ci
    strict = ri > ci
    eye = jnp.where(ri == ci, 1.0, 0.0).astype(f32)
    tri = jnp.where(causal, 1.0, 0.0).astype(f32)
    G = jnp.dot(tri, g, precision=lax.Precision.HIGHEST, preferred_element_type=f32)
    GT = lax.dot_general(g, tri, (((0,), (1,)), ((), ())), precision=lax.Precision.HIGHEST,
                         preferred_element_type=f32)

    for qh in range(DN_QK_HEADS):
        k = kn_ref[qh]
        q = qn_ref[qh]
        k16 = k.astype(bf16)
        kq = _dot_nt(jnp.concatenate([k, q], axis=0).astype(bf16), k16)
        kk = kq[:C]
        qk = kq[C:]
        kT = k.T
        for s in range(DN_V_HEADS // DN_QK_HEADS):
            h = qh * (DN_V_HEADS // DN_QK_HEADS) + s
            Gc = G[:, h:h + 1]
            Gr = GT[h:h + 1, :]
            bc = beta[:, h:h + 1]
            decay = jnp.where(causal, jnp.exp(jnp.where(causal, Gc - Gr, 0.0)), 0.0)
            M = jnp.where(strict, -(bc * kk * decay), 0.0)
            M16 = M.astype(bf16)
            Mp = _dot(M16, M16)
            T = eye + M
            for it in range(n_iter):
                Mp16 = Mp.astype(bf16)
                if it < n_iter - 1:
                    tm = _dot(jnp.concatenate([T, Mp], axis=0).astype(bf16), Mp16)
                    T = T + tm[:C]
                    Mp = tm[C:]
                else:
                    T = T + _dot(T.astype(bf16), Mp16)
            eG = jnp.exp(Gc)
            vb = v_ref[h] * bc
            kbg = k * (bc * eG)
            uw = _dot(T.astype(bf16), jnp.concatenate([vb, kbg], axis=1).astype(bf16))
            u = uw[:, :H]
            w = uw[:, H:]
            S = sout_ref[0, h]
            ws = _dot(jnp.concatenate([w, q * eG], axis=0).astype(bf16), S.astype(bf16))
            unew = u - ws[:C]
            Gl = Gc[C - 1:C, :]
            kdT = kT * jnp.exp(Gl - Gr)
            r = _dot(jnp.concatenate([qk * decay, kdT], axis=0).astype(bf16), unew.astype(bf16))
            o = ws[C:] + r[:C]
            sout_ref[0, h] = S * jnp.exp(Gl) + r[C:]
            on = _rms(o, nw_ref[...])
            z = proj_ref[0, :, DN_CONV_DIM + h * H:DN_CONV_DIM + (h + 1) * H]
            o_ref[0, :, h * H:(h + 1) * H] = (on * (z * _sigmoid(z))).astype(bf16)


def delta_mixer(proj, conv0, s0, conv_w, a_log, dt_bias, norm_w, C):
    B, L, _ = proj.shape
    n_iter = max(1, (C - 1).bit_length()) - 1
    H = DN_HEAD
    return pl.pallas_call(
        functools.partial(_delta_kernel, C, n_iter),
        grid=(B, L // C),
        in_specs=[
            pl.BlockSpec((1, C, DN_IN_PAD), lambda b, n: (b, n, 0)),
            pl.BlockSpec((1, DN_CONV - 1, DN_CONV_DIM), lambda b, n: (b, 0, 0)),
            pl.BlockSpec((1, DN_V_HEADS, H, H), lambda b, n: (b, 0, 0, 0)),
            pl.BlockSpec((DN_CONV, DN_CONV_DIM), lambda b, n: (0, 0)),
            pl.BlockSpec((1, DN_V_HEADS), lambda b, n: (0, 0)),
            pl.BlockSpec((1, DN_V_HEADS), lambda b, n: (0, 0)),
            pl.BlockSpec((1, H), lambda b, n: (0, 0)),
        ],
        out_specs=[
            pl.BlockSpec((1, C, DN_VAL_DIM), lambda b, n: (b, n, 0)),
            pl.BlockSpec((1, DN_CONV - 1, DN_CONV_DIM), lambda b, n: (b, 0, 0)),
            pl.BlockSpec((1, DN_V_HEADS, H, H), lambda b, n: (b, 0, 0, 0)),
        ],
        out_shape=[
            jax.ShapeDtypeStruct((B, L, DN_VAL_DIM), bf16),
            jax.ShapeDtypeStruct((B, DN_CONV - 1, DN_CONV_DIM), f32),
            jax.ShapeDtypeStruct((B, DN_V_HEADS, H, H), f32),
        ],
        scratch_shapes=[
            pltpu.VMEM((C + 8, DN_CONV_DIM), f32),
            pltpu.VMEM((DN_QK_HEADS, C, H), f32),
            pltpu.VMEM((DN_QK_HEADS, C, H), f32),
            pltpu.VMEM((DN_V_HEADS, C, H), f32),
        ],
        compiler_params=_params("parallel", "arbitrary"),
        name="delta_mixer",
    )(proj, conv0, s0, conv_w, a_log.reshape(1, -1), dt_bias.reshape(1, -1), norm_w.reshape(1, -1))


def _sink_softmax_pv(s, valid, sk, v16):
    s = jnp.where(valid, s, -jnp.inf)
    m = jnp.maximum(jnp.max(s, axis=-1, keepdims=True), sk)
    p = jnp.exp(s - m)
    denom = jnp.sum(p, axis=-1, keepdims=True) + jnp.exp(sk - m)
    return _dot((p / denom).astype(bf16), v16)


def _swa_prompt_kernel(sinks_ref, q_ref, kc_ref, kp_ref, vc_ref, vp_ref, o_ref):
    n = pl.program_id(1)
    W = WINDOW
    hd = SWA_HEAD_DIM
    ri = lax.broadcasted_iota(jnp.int32, (W, 2 * W), 0)
    ci = lax.broadcasted_iota(jnp.int32, (W, 2 * W), 1)
    rel = W + ri - ci
    first_col = jnp.where(n > 0, 0, W)
    valid = (rel >= 0) & (rel <= WINDOW) & (ci >= first_col)
    relf = rel.astype(f32)
    for kv in range(SWA_KV_HEADS):
        cs = slice(kv * hd, (kv + 1) * hd)
        k16 = jnp.concatenate([kp_ref[0, :, cs], kc_ref[0, :, cs]], axis=0).astype(bf16)
        v16 = jnp.concatenate([vp_ref[0, :, cs], vc_ref[0, :, cs]], axis=0).astype(bf16)
        for gi in range(SWA_GROUPS):
            h = kv * SWA_GROUPS + gi
            q16 = q_ref[0, :, h * hd:(h + 1) * hd].astype(bf16)
            s = _dot_nt(q16, k16) * SWA_SCALE - ALIBI_SLOPES[h] * relf
            o_ref[0, :, h * hd:(h + 1) * hd] = _sink_softmax_pv(s, valid, sinks_ref[h], v16).astype(bf16)


def swa_prompt_attention(qkv, sinks):
    B, L, _ = qkv.shape
    W = WINDOW
    kb = SWA_Q_DIM // SWA_KV_DIM
    prev = lambda b, n: jnp.maximum(n - 1, 0)
    return pl.pallas_call(
        _swa_prompt_kernel,
        grid=(B, L // W),
        in_specs=[
            pl.BlockSpec(memory_space=pltpu.SMEM),
            pl.BlockSpec((1, W, SWA_Q_DIM), lambda b, n: (b, n, 0)),
            pl.BlockSpec((1, W, SWA_KV_DIM), lambda b, n: (b, n, kb)),
            pl.BlockSpec((1, W, SWA_KV_DIM), lambda b, n: (b, prev(b, n), kb)),
            pl.BlockSpec((1, W, SWA_KV_DIM), lambda b, n: (b, n, kb + 1)),
            pl.BlockSpec((1, W, SWA_KV_DIM), lambda b, n: (b, prev(b, n), kb + 1)),
        ],
        out_specs=pl.BlockSpec((1, W, SWA_Q_DIM), lambda b, n: (b, n, 0)),
        out_shape=jax.ShapeDtypeStruct((B, L, SWA_Q_DIM), bf16),
        compiler_params=_params("parallel", "arbitrary"),
        name="swa_prompt",
    )(sinks, qkv, qkv, qkv, qkv, qkv)


def _swa_sample_kernel(L, R, sinks_ref, qkv_ref, ck_ref, cv_ref, o_ref):
    hd = SWA_HEAD_DIM
    G = SWA_GROUPS
    rows = G * L
    t_c = lax.broadcasted_iota(jnp.int32, (rows, R), 0) % L
    j_c = lax.broadcasted_iota(jnp.int32, (rows, R), 1)
    rel_c = R + t_c - j_c
    valid_c = (rel_c >= 0) & (rel_c <= WINDOW)
    t_n = lax.broadcasted_iota(jnp.int32, (rows, L), 0) % L
    j_n = lax.broadcasted_iota(jnp.int32, (rows, L), 1)
    rel_n = t_n - j_n
    valid_n = (rel_n >= 0) & (rel_n <= WINDOW)
    g_row = lax.broadcasted_iota(jnp.int32, (rows, 1), 0) // L
    for kv in range(SWA_KV_HEADS):
        cs = slice(kv * hd, (kv + 1) * hd)
        slope = jnp.zeros((rows, 1), f32)
        sk = jnp.zeros((rows, 1), f32)
        for gi in range(G):
            h = kv * G + gi
            slope = jnp.where(g_row == gi, ALIBI_SLOPES[h], slope)
            sk = jnp.where(g_row == gi, sinks_ref[h], sk)
        q16 = jnp.concatenate(
            [qkv_ref[0, :, (kv * G + gi) * hd:(kv * G + gi + 1) * hd] for gi in range(G)], axis=0).astype(bf16)
        kc16 = ck_ref[0, :, cs].astype(bf16)
        vc16 = cv_ref[0, :, cs].astype(bf16)
        kn16 = qkv_ref[0, :, SWA_Q_DIM + kv * hd:SWA_Q_DIM + (kv + 1) * hd].astype(bf16)
        vn16 = qkv_ref[0, :, SWA_Q_DIM + SWA_KV_DIM + kv * hd:SWA_Q_DIM + SWA_KV_DIM + (kv + 1) * hd].astype(bf16)
        s_c = _dot_nt(q16, kc16) * SWA_SCALE - slope * rel_c.astype(f32)
        s_n = _dot_nt(q16, kn16) * SWA_SCALE - slope * rel_n.astype(f32)
        s_c = jnp.where(valid_c, s_c, -jnp.inf)
        s_n = jnp.where(valid_n, s_n, -jnp.inf)
        m = jnp.maximum(jnp.maximum(jnp.max(s_c, axis=-1, keepdims=True),
                                    jnp.max(s_n, axis=-1, keepdims=True)), sk)
        p_c = jnp.exp(s_c - m)
        p_n = jnp.exp(s_n - m)
        denom = jnp.sum(p_c, axis=-1, keepdims=True) + jnp.sum(p_n, axis=-1, keepdims=True) + jnp.exp(sk - m)
        o = _dot((p_c / denom).astype(bf16), vc16) + _dot((p_n / denom).astype(bf16), vn16)
        for gi in range(G):
            h = kv * G + gi
            o_ref[0, :, h * hd:(h + 1) * hd] = o[gi * L:(gi + 1) * L].astype(bf16)


def swa_sample_attention(qkv, cache_k, cache_v, sinks):
    B, L, _ = qkv.shape
    R = cache_k.shape[1]
    return pl.pallas_call(
        functools.partial(_swa_sample_kernel, L, R),
        grid=(B,),
        in_specs=[
            pl.BlockSpec(memory_space=pltpu.SMEM),
            pl.BlockSpec((1, L, SWA_QKV_DIM), lambda b: (b, 0, 0)),
            pl.BlockSpec((1, R, SWA_KV_DIM), lambda b: (b, 0, 0)),
            pl.BlockSpec((1, R, SWA_KV_DIM), lambda b: (b, 0, 0)),
        ],
        out_specs=pl.BlockSpec((1, L, SWA_Q_DIM), lambda b: (b, 0, 0)),
        out_shape=jax.ShapeDtypeStruct((B, L, SWA_Q_DIM), bf16),
        compiler_params=_params("parallel"),
        name="swa_sample",
    )(sinks, qkv, cache_k, cache_v)


def kernel(x_prompt, x_sample, state_conv, state_delta, cache_k, cache_v, norm_mix, norm_ffn, norm_final,
           dn_w_in, dn_conv_w, dn_a_log, dn_dt_bias, dn_norm_w, dn_w_out, swa_w_qkv, swa_b_qkv, swa_sinks,
           swa_w_o, swa_b_o, ffn_w_gu, ffn_w_down):
    Bp, Lp, D = x_prompt.shape
    Bs, Ls, _ = x_sample.shape
    depth = norm_mix.shape[0]
    yp = x_prompt.reshape(Bp * Lp, D)
    ys = x_sample.reshape(Bs * Ls, D)
    R = cache_k.shape[2]

    conv_p, delta_p, kp_list, vp_list = [], [], [], []
    conv_s, delta_s, ks_list, vs_list = [], [], [], []
    zero_b = jnp.zeros((D,), f32)
    for i in range(depth):
        j = i // 2
        if i % 2 == 0:
            w_in = jnp.pad(dn_w_in[j].astype(bf16), ((0, 0), (0, DN_IN_PAD - DN_IN_DIM)))
            w_out = dn_w_out[j].astype(bf16)
            zero_in = jnp.zeros((DN_IN_PAD,), f32)
            wts = (dn_conv_w[j], dn_a_log[j], dn_dt_bias[j], dn_norm_w[j])
            proj = rms_matmul(yp, norm_mix[i], w_in, zero_in, 896).reshape(Bp, Lp, DN_IN_PAD)
            conv0 = jnp.zeros((Bp, DN_CONV - 1, DN_CONV_DIM), f32)
            s0 = jnp.zeros((Bp, DN_V_HEADS, DN_HEAD, DN_HEAD), f32)
            o, cb, sn = delta_mixer(proj, conv0, s0, *wts, C=min(DN_CHUNK, Lp))
            yp = matmul_resid(o.reshape(Bp * Lp, DN_VAL_DIM), w_out, zero_b, yp)
            conv_p.append(cb)
            delta_p.append(sn)
            proj = rms_matmul(ys, norm_mix[i], w_in, zero_in, 896).reshape(Bs, Ls, DN_IN_PAD)
            o, cb, sn = delta_mixer(proj, state_conv[j], state_delta[j], *wts, C=Ls)
            ys = matmul_resid(o.reshape(Bs * Ls, DN_VAL_DIM), w_out, zero_b, ys)
            conv_s.append(cb)
            delta_s.append(sn)
        else:
            w_qkv = swa_w_qkv[j].astype(bf16)
            w_o = swa_w_o[j].astype(bf16)
            qkv = rms_matmul(yp, norm_mix[i], w_qkv, swa_b_qkv[j], 768).reshape(Bp, Lp, SWA_QKV_DIM)
            o = swa_prompt_attention(qkv, swa_sinks[j])
            yp = matmul_resid(o.reshape(Bp * Lp, SWA_Q_DIM), w_o, swa_b_o[j], yp)
            Rp = min(WINDOW, Lp)
            kp_list.append(qkv[:, Lp - Rp:, SWA_Q_DIM:SWA_Q_DIM + SWA_KV_DIM].reshape(Bp, Rp, SWA_KV_HEADS, SWA_HEAD_DIM))
            vp_list.append(qkv[:, Lp - Rp:, SWA_Q_DIM + SWA_KV_DIM:].reshape(Bp, Rp, SWA_KV_HEADS, SWA_HEAD_DIM))
            qkv = rms_matmul(ys, norm_mix[i], w_qkv, swa_b_qkv[j], 768).reshape(Bs, Ls, SWA_QKV_DIM)
            o = swa_sample_attention(qkv, cache_k[j].reshape(Bs, R, SWA_KV_DIM), cache_v[j].reshape(Bs, R, SWA_KV_DIM),
                                     swa_sinks[j])
            ys = matmul_resid(o.reshape(Bs * Ls, SWA_Q_DIM), w_o, swa_b_o[j], ys)
            ks_list.append(qkv[:, :, SWA_Q_DIM:SWA_Q_DIM + SWA_KV_DIM].reshape(Bs, Ls, SWA_KV_HEADS, SWA_HEAD_DIM))
            vs_list.append(qkv[:, :, SWA_Q_DIM + SWA_KV_DIM:].reshape(Bs, Ls, SWA_KV_HEADS, SWA_HEAD_DIM))
        w_gu = ffn_w_gu[i].astype(bf16)
        w_down = ffn_w_down[i].astype(bf16)
        last = i == depth - 1
        yp = ffn(yp, norm_ffn[i], w_gu, w_down, norm_final, last)
        ys = ffn(ys, norm_ffn[i], w_gu, w_down, norm_final, last)

    return (yp.reshape(Bp, Lp, D), ys.reshape(Bs, Ls, D),
            jnp.stack(conv_p), jnp.stack(delta_p), jnp.stack(kp_list), jnp.stack(vp_list),
            jnp.stack(conv_s), jnp.stack(delta_s), jnp.stack(ks_list), jnp.stack(vs_list))
```

```python
import functools
import math

import numpy as np
import jax
import jax.numpy as jnp
from jax import lax
from jax.experimental import pallas as pl
from jax.experimental.pallas import tpu as pltpu

f32 = jnp.float32
bf16 = jnp.bfloat16

D_MODEL = 1024
DN_QK_HEADS = 8
DN_V_HEADS = 16
DN_HEAD = 128
DN_KEY_DIM = DN_QK_HEADS * DN_HEAD
DN_VAL_DIM = DN_V_HEADS * DN_HEAD
DN_CONV_DIM = 2 * DN_KEY_DIM + DN_VAL_DIM
DN_CONV = 4
DN_IN_DIM = DN_CONV_DIM + DN_VAL_DIM + 2 * DN_V_HEADS
DN_IN_PAD = 6272
SWA_HEADS = 16
SWA_KV_HEADS = 4
SWA_GROUPS = SWA_HEADS // SWA_KV_HEADS
SWA_HEAD_DIM = 64
SWA_Q_DIM = SWA_HEADS * SWA_HEAD_DIM
SWA_KV_DIM = SWA_KV_HEADS * SWA_HEAD_DIM
SWA_QKV_DIM = SWA_Q_DIM + 2 * SWA_KV_DIM
SWA_SCALE = SWA_HEAD_DIM ** -0.5
WINDOW = 128
D_FF = 2816
RMS_EPS = 1e-6
L2_EPS = 1e-6
ALIBI_SLOPES = tuple(float(2.0 ** (-8.0 * (i + 1) / SWA_HEADS)) for i in range(SWA_HEADS))

VMEM_LIMIT_BYTES = 56 * 1024 * 1024
DN_CHUNK = 64


def _params(*sem):
    return pltpu.CompilerParams(dimension_semantics=sem, vmem_limit_bytes=VMEM_LIMIT_BYTES)


def _sigmoid(x):
    return 1.0 / (1.0 + jnp.exp(-x))


def _rms(x, w):
    return x * lax.rsqrt(jnp.mean(x * x, axis=-1, keepdims=True) + RMS_EPS) * w


def _dot(a, b):
    return jnp.dot(a, b, preferred_element_type=f32)


def _dot_nt(a, b):
    return lax.dot_general(a, b, (((1,), (1,)), ((), ())), preferred_element_type=f32)


def _rms_matmul_kernel(x_ref, nw_ref, w_ref, b_ref, o_ref, xn_ref):
    @pl.when(pl.program_id(1) == 0)
    def _():
        xn_ref[...] = _rms(x_ref[...], nw_ref[...]).astype(bf16)

    o_ref[...] = _dot(xn_ref[...], w_ref[...]) + b_ref[...]


def rms_matmul(x, nw, w, b, tn):
    m, d = x.shape
    n = w.shape[1]
    tm = min(m, 1024)
    return pl.pallas_call(
        _rms_matmul_kernel,
        grid=(m // tm, n // tn),
        in_specs=[
            pl.BlockSpec((tm, d), lambda i, j: (i, 0)),
            pl.BlockSpec((1, d), lambda i, j: (0, 0)),
            pl.BlockSpec((d, tn), lambda i, j: (0, j)),
            pl.BlockSpec((1, tn), lambda i, j: (0, j)),
        ],
        out_specs=pl.BlockSpec((tm, tn), lambda i, j: (i, j)),
        out_shape=jax.ShapeDtypeStruct((m, n), f32),
        scratch_shapes=[pltpu.VMEM((tm, d), bf16)],
        compiler_params=_params("parallel", "arbitrary"),
        name="rms_matmul",
    )(x, nw.reshape(1, d), w, b.reshape(1, n))


def _matmul_resid_kernel(a_ref, w_ref, b_ref, r_ref, o_ref):
    o_ref[...] = r_ref[...] + _dot(a_ref[...], w_ref[...]) + b_ref[...]


def matmul_resid(a, w, b, resid):
    m, k = a.shape
    n = w.shape[1]
    tm = min(m, 512)
    return pl.pallas_call(
        _matmul_resid_kernel,
        grid=(m // tm,),
        in_specs=[
            pl.BlockSpec((tm, k), lambda i: (i, 0)),
            pl.BlockSpec((k, n), lambda i: (0, 0)),
            pl.BlockSpec((1, n), lambda i: (0, 0)),
            pl.BlockSpec((tm, n), lambda i: (i, 0)),
        ],
        out_specs=pl.BlockSpec((tm, n), lambda i: (i, 0)),
        out_shape=jax.ShapeDtypeStruct((m, n), f32),
        compiler_params=_params("parallel"),
        name="matmul_resid",
    )(a, w, b.reshape(1, n), resid)


def _ffn_kernel(final_norm, x_ref, nw_ref, wg_ref, wu_ref, wd_ref, nf_ref, o_ref, xn_ref, acc_ref):
    j = pl.program_id(1)

    @pl.when(j == 0)
    def _():
        xn_ref[...] = _rms(x_ref[...], nw_ref[...]).astype(bf16)
        acc_ref[...] = jnp.zeros_like(acc_ref)

    xn = xn_ref[...]
    g = _dot(xn, wg_ref[...])
    u = _dot(xn, wu_ref[...])
    h = (g * _sigmoid(g) * u).astype(bf16)
    acc_ref[...] += _dot(h, wd_ref[...])

    @pl.when(j == pl.num_programs(1) - 1)
    def _():
        y = x_ref[...] + acc_ref[...]
        if final_norm:
            y = _rms(y, nf_ref[...])
        o_ref[...] = y


def ffn(x, nw, w_gu, w_down, nf, final_norm):
    m, d = x.shape
    tm = min(m, 512)
    tf = D_FF // 2
    nj = D_FF // tf
    return pl.pallas_call(
        functools.partial(_ffn_kernel, final_norm),
        grid=(m // tm, nj),
        in_specs=[
            pl.BlockSpec((tm, d), lambda i, j: (i, 0)),
            pl.BlockSpec((1, d), lambda i, j: (0, 0)),
            pl.BlockSpec((d, tf), lambda i, j: (0, j)),
            pl.BlockSpec((d, tf), lambda i, j: (0, j + nj)),
            pl.BlockSpec((tf, d), lambda i, j: (j, 0)),
            pl.BlockSpec((1, d), lambda i, j: (0, 0)),
        ],
        out_specs=pl.BlockSpec((tm, d), lambda i, j: (i, 0)),
        out_shape=jax.ShapeDtypeStruct((m, d), f32),
        scratch_shapes=[pltpu.VMEM((tm, d), bf16), pltpu.VMEM((tm, d), f32)],
        compiler_params=_params("parallel", "arbitrary"),
        name="ffn",
    )(x, nw.reshape(1, d), w_gu, w_gu, w_down, nf.reshape(1, d))


def _delta_kernel(C, n_iter, proj_ref, conv0_ref, s0_ref, cw_ref, alog_ref, dtb_ref, nw_ref,
                  o_ref, convout_ref, sout_ref, full_ref, qn_ref, kn_ref, v_ref):
    n = pl.program_id(1)
    H = DN_HEAD

    @pl.when(n == 0)
    def _():
        full_ref[5:8, :] = conv0_ref[0]
        sout_ref[...] = s0_ref[...]

    full_ref[8:8 + C, :] = proj_ref[0, :, 0:DN_CONV_DIM]

    for cb in range(DN_CONV_DIM // H):
        cols = slice(cb * H, (cb + 1) * H)
        w = cw_ref[:, cols]
        y = (full_ref[5:5 + C, cols] * w[0:1] + full_ref[6:6 + C, cols] * w[1:2]
             + full_ref[7:7 + C, cols] * w[2:3] + full_ref[8:8 + C, cols] * w[3:4])
        y = y * _sigmoid(y)
        if cb < 2 * DN_QK_HEADS:
            inv = lax.rsqrt(jnp.sum(y * y, axis=-1, keepdims=True) + L2_EPS)
            if cb < DN_QK_HEADS:
                qn_ref[cb] = y * (inv * (H ** -0.5))
            else:
                kn_ref[cb - DN_QK_HEADS] = y * inv
        else:
            v_ref[cb - 2 * DN_QK_HEADS] = y

    tail = full_ref[C + 5:C + 8, :]
    full_ref[5:8, :] = tail

    @pl.when(n == pl.num_programs(1) - 1)
    def _():
        convout_ref[0] = tail

    ba = proj_ref[0, :, DN_CONV_DIM + DN_VAL_DIM:DN_IN_PAD]
    beta = _sigmoid(ba[:, 0:DN_V_HEADS])
    xa = ba[:, DN_V_HEADS:2 * DN_V_HEADS] + dtb_ref[...]
    softplus = jnp.maximum(xa, 0.0) + jnp.log(1.0 + jnp.exp(-jnp.abs(xa)))
    g = -jnp.exp(alog_ref[...]) * softplus

    ri = lax.broadcasted_iota(jnp.int32, (C, C), 0)
    ci = lax.broadcasted_iota(jnp.int32, (C, C), 1)
    causal = ri >= ci
    strict = ri > ci
    eye = jnp.where(ri == ci, 1.0, 0.0).astype(f32)
    tri = jnp.where(causal, 1.0, 0.0).astype(f32)
    G = jnp.dot(tri, g, precision=lax.Precision.HIGHEST, preferred_element_type=f32)
    GT = lax.dot_general(g, tri, (((0,), (1,)), ((), ())), precision=lax.Precision.HIGHEST,
                         preferred_element_type=f32)

    NH = DN_V_HEADS
    per = DN_V_HEADS // DN_QK_HEADS
    ks = [kn_ref[qh] for qh in range(DN_QK_HEADS)]
    qs = [qn_ref[qh] for qh in range(DN_QK_HEADS)]
    kqs = [_dot_nt(jnp.concatenate([ks[qh], qs[qh]], axis=0).astype(bf16), ks[qh].astype(bf16))
           for qh in range(DN_QK_HEADS)]
    kTs = [ks[qh].T for qh in range(DN_QK_HEADS)]
    Gc = [G[:, h:h + 1] for h in range(NH)]
    Gr = [GT[h:h + 1, :] for h in range(NH)]
    bc = [beta[:, h:h + 1] for h in range(NH)]
    decay = [jnp.where(causal, jnp.exp(jnp.where(causal, Gc[h] - Gr[h], 0.0)), 0.0) for h in range(NH)]
    M = [jnp.where(strict, -(bc[h] * kqs[h // per][:C] * decay[h]), 0.0) for h in range(NH)]
    T = [eye + M[h] for h in range(NH)]
    Mp = [_dot(M[h].astype(bf16), M[h].astype(bf16)) for h in range(NH)]
    for it in range(n_iter):
        if it < n_iter - 1:
            tm = [_dot(jnp.concatenate([T[h], Mp[h]], axis=0).astype(bf16), Mp[h].astype(bf16)) for h in range(NH)]
            T = [T[h] + tm[h][:C] for h in range(NH)]
            Mp = [tm[h][C:] for h in range(NH)]
        else:
            tm = [_dot(T[h].astype(bf16), Mp[h].astype(bf16)) for h in range(NH)]
            T = [T[h] + tm[h] for h in range(NH)]
    eG = [jnp.exp(Gc[h]) for h in range(NH)]
    uw = [_dot(T[h].astype(bf16),
               jnp.concatenate([v_ref[h] * bc[h], ks[h // per] * (bc[h] * eG[h])], axis=1).astype(bf16))
          for h in range(NH)]
    S = [sout_ref[0, h] for h in range(NH)]
    ws = [_dot(jnp.concatenate([uw[h][:, H:], qs[h // per] * eG[h]], axis=0).astype(bf16), S[h].astype(bf16))
          for h in range(NH)]
    unew = [uw[h][:, :H] - ws[h][:C] for h in range(NH)]
    Gl = [Gc[h][C - 1:C, :] for h in range(NH)]
    r = [_dot(jnp.concatenate([kqs[h // per][C:] * decay[h], kTs[h // per] * jnp.exp(Gl[h] - Gr[h])],
                              axis=0).astype(bf16), unew[h].astype(bf16))
         for h in range(NH)]
    for h in range(NH):
        sout_ref[0, h] = S[h] * jnp.exp(Gl[h]) + r[h][C:]
        on = _rms(ws[h][C:] + r[h][:C], nw_ref[...])
        z = proj_ref[0, :, DN_CONV_DIM + h * H:DN_CONV_DIM + (h + 1) * H]
        o_ref[0, :, h * H:(h + 1) * H] = (on * (z * _sigmoid(z))).astype(bf16)


def delta_mixer(proj, conv0, s0, conv_w, a_log, dt_bias, norm_w, C):
    B, L, _ = proj.shape
    n_iter = max(1, (C - 1).bit_length()) - 1
    H = DN_HEAD
    return pl.pallas_call(
        functools.partial(_delta_kernel, C, n_iter),
        grid=(B, L // C),
        in_specs=[
            pl.BlockSpec((1, C, DN_IN_PAD), lambda b, n: (b, n, 0)),
            pl.BlockSpec((1, DN_CONV - 1, DN_CONV_DIM), lambda b, n: (b, 0, 0)),
            pl.BlockSpec((1, DN_V_HEADS, H, H), lambda b, n: (b, 0, 0, 0)),
            pl.BlockSpec((DN_CONV, DN_CONV_DIM), lambda b, n: (0, 0)),
            pl.BlockSpec((1, DN_V_HEADS), lambda b, n: (0, 0)),
            pl.BlockSpec((1, DN_V_HEADS), lambda b, n: (0, 0)),
            pl.BlockSpec((1, H), lambda b, n: (0, 0)),
        ],
        out_specs=[
            pl.BlockSpec((1, C, DN_VAL_DIM), lambda b, n: (b, n, 0)),
            pl.BlockSpec((1, DN_CONV - 1, DN_CONV_DIM), lambda b, n: (b, 0, 0)),
            pl.BlockSpec((1, DN_V_HEADS, H, H), lambda b, n: (b, 0, 0, 0)),
        ],
        out_shape=[
            jax.ShapeDtypeStruct((B, L, DN_VAL_DIM), bf16),
            jax.ShapeDtypeStruct((B, DN_CONV - 1, DN_CONV_DIM), f32),
            jax.ShapeDtypeStruct((B, DN_V_HEADS, H, H), f32),
        ],
        scratch_shapes=[
            pltpu.VMEM((C + 8, DN_CONV_DIM), f32),
            pltpu.VMEM((DN_QK_HEADS, C, H), f32),
            pltpu.VMEM((DN_QK_HEADS, C, H), f32),
            pltpu.VMEM((DN_V_HEADS, C, H), f32),
        ],
        compiler_params=_params("parallel", "arbitrary"),
        name="delta_mixer",
    )(proj, conv0, s0, conv_w, a_log.reshape(1, -1), dt_bias.reshape(1, -1), norm_w.reshape(1, -1))


def _sink_softmax_pv(s, valid, sk, v16):
    s = jnp.where(valid, s, -jnp.inf)
    m = jnp.maximum(jnp.max(s, axis=-1, keepdims=True), sk)
    p = jnp.exp(s - m)
    denom = jnp.sum(p, axis=-1, keepdims=True) + jnp.exp(sk - m)
    return _dot((p / denom).astype(bf16), v16)


def _swa_prompt_kernel(sinks_ref, q_ref, kc_ref, kp_ref, vc_ref, vp_ref, o_ref):
    n = pl.program_id(1)
    W = WINDOW
    hd = SWA_HEAD_DIM
    ri = lax.broadcasted_iota(jnp.int32, (W, 2 * W), 0)
    ci = lax.broadcasted_iota(jnp.int32, (W, 2 * W), 1)
    rel = W + ri - ci
    first_col = jnp.where(n > 0, 0, W)
    valid = (rel >= 0) & (rel <= WINDOW) & (ci >= first_col)
    relf = rel.astype(f32)
    for kv in range(SWA_KV_HEADS):
        cs = slice(kv * hd, (kv + 1) * hd)
        k16 = jnp.concatenate([kp_ref[0, :, cs], kc_ref[0, :, cs]], axis=0).astype(bf16)
        v16 = jnp.concatenate([vp_ref[0, :, cs], vc_ref[0, :, cs]], axis=0).astype(bf16)
        for gi in range(SWA_GROUPS):
            h = kv * SWA_GROUPS + gi
            q16 = q_ref[0, :, h * hd:(h + 1) * hd].astype(bf16)
            s = _dot_nt(q16, k16) * SWA_SCALE - ALIBI_SLOPES[h] * relf
            o_ref[0, :, h * hd:(h + 1) * hd] = _sink_softmax_pv(s, valid, sinks_ref[h], v16).astype(bf16)


def swa_prompt_attention(qkv, sinks):
    B, L, _ = qkv.shape
    W = WINDOW
    kb = SWA_Q_DIM // SWA_KV_DIM
    prev = lambda b, n: jnp.maximum(n - 1, 0)
    return pl.pallas_call(
        _swa_prompt_kernel,
        grid=(B, L // W),
        in_specs=[
            pl.BlockSpec(memory_space=pltpu.SMEM),
            pl.BlockSpec((1, W, SWA_Q_DIM), lambda b, n: (b, n, 0)),
            pl.BlockSpec((1, W, SWA_KV_DIM), lambda b, n: (b, n, kb)),
            pl.BlockSpec((1, W, SWA_KV_DIM), lambda b, n: (b, prev(b, n), kb)),
            pl.BlockSpec((1, W, SWA_KV_DIM), lambda b, n: (b, n, kb + 1)),
            pl.BlockSpec((1, W, SWA_KV_DIM), lambda b, n: (b, prev(b, n), kb + 1)),
        ],
        out_specs=pl.BlockSpec((1, W, SWA_Q_DIM), lambda b, n: (b, n, 0)),
        out_shape=jax.ShapeDtypeStruct((B, L, SWA_Q_DIM), bf16),
        compiler_params=_params("parallel", "arbitrary"),
        name="swa_prompt",
    )(sinks, qkv, qkv, qkv, qkv, qkv)


def _swa_sample_kernel(L, R, sinks_ref, qkv_ref, ck_ref, cv_ref, o_ref):
    hd = SWA_HEAD_DIM
    G = SWA_GROUPS
    rows = G * L
    t_c = lax.broadcasted_iota(jnp.int32, (rows, R), 0) % L
    j_c = lax.broadcasted_iota(jnp.int32, (rows, R), 1)
    rel_c = R + t_c - j_c
    valid_c = (rel_c >= 0) & (rel_c <= WINDOW)
    t_n = lax.broadcasted_iota(jnp.int32, (rows, L), 0) % L
    j_n = lax.broadcasted_iota(jnp.int32, (rows, L), 1)
    rel_n = t_n - j_n
    valid_n = (rel_n >= 0) & (rel_n <= WINDOW)
    g_row = lax.broadcasted_iota(jnp.int32, (rows, 1), 0) // L
    for kv in range(SWA_KV_HEADS):
        cs = slice(kv * hd, (kv + 1) * hd)
        slope = jnp.zeros((rows, 1), f32)
        sk = jnp.zeros((rows, 1), f32)
        for gi in range(G):
            h = kv * G + gi
            slope = jnp.where(g_row == gi, ALIBI_SLOPES[h], slope)
            sk = jnp.where(g_row == gi, sinks_ref[h], sk)
        q16 = jnp.concatenate(
            [qkv_ref[0, :, (kv * G + gi) * hd:(kv * G + gi + 1) * hd] for gi in range(G)], axis=0).astype(bf16)
        kc16 = ck_ref[0, :, cs].astype(bf16)
        vc16 = cv_ref[0, :, cs].astype(bf16)
        kn16 = qkv_ref[0, :, SWA_Q_DIM + kv * hd:SWA_Q_DIM + (kv + 1) * hd].astype(bf16)
        vn16 = qkv_ref[0, :, SWA_Q_DIM + SWA_KV_DIM + kv * hd:SWA_Q_DIM + SWA_KV_DIM + (kv + 1) * hd].astype(bf16)
        s_c = _dot_nt(q16, kc16) * SWA_SCALE - slope * rel_c.astype(f32)
        s_n = _dot_nt(q16, kn16) * SWA_SCALE - slope * rel_n.astype(f32)
        s_c = jnp.where(valid_c, s_c, -jnp.inf)
        s_n = jnp.where(valid_n, s_n, -jnp.inf)
        m = jnp.maximum(jnp.maximum(jnp.max(s_c, axis=-1, keepdims=True),
                                    jnp.max(s_n, axis=-1, keepdims=True)), sk)
        p_c = jnp.exp(s_c - m)
        p_n = jnp.exp(s_n - m)
        denom = jnp.sum(p_c, axis=-1, keepdims=True) + jnp.sum(p_n, axis=-1, keepdims=True) + jnp.exp(sk - m)
        o = _dot((p_c / denom).astype(bf16), vc16) + _dot((p_n / denom).astype(bf16), vn16)
        for gi in range(G):
            h = kv * G + gi
            o_ref[0, :, h * hd:(h + 1) * hd] = o[gi * L:(gi + 1) * L].astype(bf16)


def swa_sample_attention(qkv, cache_k, cache_v, sinks):
    B, L, _ = qkv.shape
    R = cache_k.shape[1]
    return pl.pallas_call(
        functools.partial(_swa_sample_kernel, L, R),
        grid=(B,),
        in_specs=[
            pl.BlockSpec(memory_space=pltpu.SMEM),
            pl.BlockSpec((1, L, SWA_QKV_DIM), lambda b: (b, 0, 0)),
            pl.BlockSpec((1, R, SWA_KV_DIM), lambda b: (b, 0, 0)),
            pl.BlockSpec((1, R, SWA_KV_DIM), lambda b: (b, 0, 0)),
        ],
        out_specs=pl.BlockSpec((1, L, SWA_Q_DIM), lambda b: (b, 0, 0)),
        out_shape=jax.ShapeDtypeStruct((B, L, SWA_Q_DIM), bf16),
        compiler_params=_params("parallel"),
        name="swa_sample",
    )(sinks, qkv, cache_k, cache_v)


def kernel(x_prompt, x_sample, state_conv, state_delta, cache_k, cache_v, norm_mix, norm_ffn, norm_final,
           dn_w_in, dn_conv_w, dn_a_log, dn_dt_bias, dn_norm_w, dn_w_out, swa_w_qkv, swa_b_qkv, swa_sinks,
           swa_w_o, swa_b_o, ffn_w_gu, ffn_w_down):
    Bp, Lp, D = x_prompt.shape
    Bs, Ls, _ = x_sample.shape
    depth = norm_mix.shape[0]
    yp = x_prompt.reshape(Bp * Lp, D)
    ys = x_sample.reshape(Bs * Ls, D)
    R = cache_k.shape[2]

    conv_p, delta_p, kp_list, vp_list = [], [], [], []
    conv_s, delta_s, ks_list, vs_list = [], [], [], []
    zero_b = jnp.zeros((D,), f32)
    for i in range(depth):
        j = i // 2
        if i % 2 == 0:
            w_in = jnp.pad(dn_w_in[j].astype(bf16), ((0, 0), (0, DN_IN_PAD - DN_IN_DIM)))
            w_out = dn_w_out[j].astype(bf16)
            zero_in = jnp.zeros((DN_IN_PAD,), f32)
            wts = (dn_conv_w[j], dn_a_log[j], dn_dt_bias[j], dn_norm_w[j])
            proj = rms_matmul(yp, norm_mix[i], w_in, zero_in, 896).reshape(Bp, Lp, DN_IN_PAD)
            conv0 = jnp.zeros((Bp, DN_CONV - 1, DN_CONV_DIM), f32)
            s0 = jnp.zeros((Bp, DN_V_HEADS, DN_HEAD, DN_HEAD), f32)
            o, cb, sn = delta_mixer(proj, conv0, s0, *wts, C=min(DN_CHUNK, Lp))
            yp = matmul_resid(o.reshape(Bp * Lp, DN_VAL_DIM), w_out, zero_b, yp)
            conv_p.append(cb)
            delta_p.append(sn)
            proj = rms_matmul(ys, norm_mix[i], w_in, zero_in, 896).reshape(Bs, Ls, DN_IN_PAD)
            o, cb, sn = delta_mixer(proj, state_conv[j], state_delta[j], *wts, C=Ls)
            ys = matmul_resid(o.reshape(Bs * Ls, DN_VAL_DIM), w_out, zero_b, ys)
            conv_s.append(cb)
            delta_s.append(sn)
        else:
            w_qkv = swa_w_qkv[j].astype(bf16)
            w_o = swa_w_o[j].astype(bf16)
            qkv = rms_matmul(yp, norm_mix[i], w_qkv, swa_b_qkv[j], 768).reshape(Bp, Lp, SWA_QKV_DIM)
            o = swa_prompt_attention(qkv, swa_sinks[j])
            yp = matmul_resid(o.reshape(Bp * Lp, SWA_Q_DIM), w_o, swa_b_o[j], yp)
            Rp = min(WINDOW, Lp)
            kp_list.append(qkv[:, Lp - Rp:, SWA_Q_DIM:SWA_Q_DIM + SWA_KV_DIM].reshape(Bp, Rp, SWA_KV_HEADS, SWA_HEAD_DIM))
            vp_list.append(qkv[:, Lp - Rp:, SWA_Q_DIM + SWA_KV_DIM:].reshape(Bp, Rp, SWA_KV_HEADS, SWA_HEAD_DIM))
            qkv = rms_matmul(ys, norm_mix[i], w_qkv, swa_b_qkv[j], 768).reshape(Bs, Ls, SWA_QKV_DIM)
            o = swa_sample_attention(qkv, cache_k[j].reshape(Bs, R, SWA_KV_DIM), cache_v[j].reshape(Bs, R, SWA_KV_DIM),
                                     swa_sinks[j])
            ys = matmul_resid(o.reshape(Bs * Ls, SWA_Q_DIM), w_o, swa_b_o[j], ys)
            ks_list.append(qkv[:, :, SWA_Q_DIM:SWA_Q_DIM + SWA_KV_DIM].reshape(Bs, Ls, SWA_KV_HEADS, SWA_HEAD_DIM))
            vs_list.append(qkv[:, :, SWA_Q_DIM + SWA_KV_DIM:].reshape(Bs, Ls, SWA_KV_HEADS, SWA_HEAD_DIM))
        w_gu = ffn_w_gu[i].astype(bf16)
        w_down = ffn_w_down[i].astype(bf16)
        last = i == depth - 1
        yp = ffn(yp, norm_ffn[i], w_gu, w_down, norm_final, last)
        ys = ffn(ys, norm_ffn[i], w_gu, w_down, norm_final, last)

    return (yp.reshape(Bp, Lp, D), ys.reshape(Bs, Ls, D),
            jnp.stack(conv_p), jnp.stack(delta_p), jnp.stack(kp_list), jnp.stack(vp_list),
            jnp.stack(conv_s), jnp.stack(delta_s), jnp.stack(ks_list), jnp.stack(vs_list))
```

```python
import functools

import jax
import jax.numpy as jnp
from jax import lax
from jax.experimental import pallas as pl
from jax.experimental.pallas import tpu as pltpu

f32 = jnp.float32
bf16 = jnp.bfloat16

D_MODEL = 1024
DN_QK_HEADS = 8
DN_V_HEADS = 16
DN_HEAD = 128
DN_KEY_DIM = DN_QK_HEADS * DN_HEAD
DN_VAL_DIM = DN_V_HEADS * DN_HEAD
DN_CONV_DIM = 2 * DN_KEY_DIM + DN_VAL_DIM
DN_CONV = 4
DN_IN_DIM = DN_CONV_DIM + DN_VAL_DIM + 2 * DN_V_HEADS
DN_IN_PAD = 6272
SWA_HEADS = 16
SWA_KV_HEADS = 4
SWA_GROUPS = SWA_HEADS // SWA_KV_HEADS
SWA_HEAD_DIM = 64
SWA_Q_DIM = SWA_HEADS * SWA_HEAD_DIM
SWA_KV_DIM = SWA_KV_HEADS * SWA_HEAD_DIM
SWA_QKV_DIM = SWA_Q_DIM + 2 * SWA_KV_DIM
SWA_SCALE = SWA_HEAD_DIM ** -0.5
WINDOW = 128
D_FF = 2816
RMS_EPS = 1e-6
L2_EPS = 1e-6
ALIBI_SLOPES = tuple(float(2.0 ** (-8.0 * (i + 1) / SWA_HEADS)) for i in range(SWA_HEADS))

VMEM_LIMIT_BYTES = 56 * 1024 * 1024
DN_CHUNK = 64
SWA_SAMPLE_SEQS = 8


def _params(*sem):
    return pltpu.CompilerParams(dimension_semantics=sem, vmem_limit_bytes=VMEM_LIMIT_BYTES)


def _sigmoid(x):
    return 1.0 / (1.0 + jnp.exp(-x))


def _rms(x, w):
    return x * lax.rsqrt(jnp.mean(x * x, axis=-1, keepdims=True) + RMS_EPS) * w


def _dot(a, b):
    return jnp.dot(a, b, preferred_element_type=f32)


def _dot_nt(a, b):
    return lax.dot_general(a, b, (((1,), (1,)), ((), ())), preferred_element_type=f32)


def _dot_tn(a, b):
    return lax.dot_general(a, b, (((0,), (0,)), ((), ())), preferred_element_type=f32)


def _rms_matmul_kernel(x_ref, nw_ref, w_ref, b_ref, o_ref, xn_ref):
    @pl.when(pl.program_id(1) == 0)
    def _():
        xn_ref[...] = _rms(x_ref[...], nw_ref[...]).astype(bf16)

    o_ref[...] = _dot(xn_ref[...], w_ref[...]) + b_ref[...]


def rms_matmul(x, nw, w_stack, layer, b, tn):
    m, d = x.shape
    n = w_stack.shape[2]
    tm = min(m, 1024)
    return pl.pallas_call(
        _rms_matmul_kernel,
        grid=(m // tm, n // tn),
        in_specs=[
            pl.BlockSpec((tm, d), lambda i, j: (i, 0)),
            pl.BlockSpec((1, d), lambda i, j: (0, 0)),
            pl.BlockSpec((None, d, tn), lambda i, j: (layer, 0, j)),
            pl.BlockSpec((1, tn), lambda i, j: (0, j)),
        ],
        out_specs=pl.BlockSpec((tm, tn), lambda i, j: (i, j)),
        out_shape=jax.ShapeDtypeStruct((m, n), f32),
        scratch_shapes=[pltpu.VMEM((tm, d), bf16)],
        compiler_params=_params("parallel", "arbitrary"),
        name="rms_matmul",
    )(x, nw.reshape(1, d), w_stack, b.reshape(1, n))


def _matmul_resid_kernel(a_ref, w_ref, b_ref, r_ref, o_ref):
    o_ref[...] = r_ref[...] + _dot(a_ref[...], w_ref[...]) + b_ref[...]


def matmul_resid(a, w_stack, layer, b, resid):
    m, k = a.shape
    n = w_stack.shape[2]
    tm = min(m, 512)
    return pl.pallas_call(
        _matmul_resid_kernel,
        grid=(m // tm,),
        in_specs=[
            pl.BlockSpec((tm, k), lambda i: (i, 0)),
            pl.BlockSpec((None, k, n), lambda i: (layer, 0, 0)),
            pl.BlockSpec((1, n), lambda i: (0, 0)),
            pl.BlockSpec((tm, n), lambda i: (i, 0)),
        ],
        out_specs=pl.BlockSpec((tm, n), lambda i: (i, 0)),
        out_shape=jax.ShapeDtypeStruct((m, n), f32),
        compiler_params=_params("parallel"),
        name="matmul_resid",
    )(a, w_stack, b.reshape(1, n), resid)


def _ffn_kernel(final_norm, x_ref, nw_ref, wg_ref, wu_ref, wd_ref, nf_ref, o_ref, xn_ref, acc_ref):
    j = pl.program_id(1)

    @pl.when(j == 0)
    def _():
        xn_ref[...] = _rms(x_ref[...], nw_ref[...]).astype(bf16)
        acc_ref[...] = jnp.zeros_like(acc_ref)

    xn = xn_ref[...]
    g = _dot(xn, wg_ref[...])
    u = _dot(xn, wu_ref[...])
    h = (g * _sigmoid(g) * u).astype(bf16)
    acc_ref[...] += _dot(h, wd_ref[...])

    @pl.when(j == pl.num_programs(1) - 1)
    def _():
        y = x_ref[...] + acc_ref[...]
        if final_norm:
            y = _rms(y, nf_ref[...])
        o_ref[...] = y


def ffn(x, nw, wgu_stack, wd_stack, layer, nf, final_norm):
    m, d = x.shape
    tm = min(m, 512)
    tf = D_FF // 2
    nj = D_FF // tf
    return pl.pallas_call(
        functools.partial(_ffn_kernel, final_norm),
        grid=(m // tm, nj),
        in_specs=[
            pl.BlockSpec((tm, d), lambda i, j: (i, 0)),
            pl.BlockSpec((1, d), lambda i, j: (0, 0)),
            pl.BlockSpec((None, d, tf), lambda i, j: (layer, 0, j)),
            pl.BlockSpec((None, d, tf), lambda i, j: (layer, 0, j + nj)),
            pl.BlockSpec((None, tf, d), lambda i, j: (layer, j, 0)),
            pl.BlockSpec((1, d), lambda i, j: (0, 0)),
        ],
        out_specs=pl.BlockSpec((tm, d), lambda i, j: (i, 0)),
        out_shape=jax.ShapeDtypeStruct((m, d), f32),
        scratch_shapes=[pltpu.VMEM((tm, d), bf16), pltpu.VMEM((tm, d), f32)],
        compiler_params=_params("parallel", "arbitrary"),
        name="ffn",
    )(x, nw.reshape(1, d), wgu_stack, wgu_stack, wd_stack, nf.reshape(1, d))


def _split3(x):
    hi = x.astype(bf16)
    r1 = x - hi.astype(f32)
    mid = r1.astype(bf16)
    lo = (r1 - mid.astype(f32)).astype(bf16)
    return hi, mid, lo


def _delta_kernel(C, n_iter, has_init, n_alias, *refs):
    proj_ref = refs[0]
    pos = 1
    if has_init:
        conv0_ref, s0_ref = refs[1:3]
        pos = 3
    cw_ref, alog_ref, dtb_ref, nw_ref = refs[pos:pos + 4]
    pos += 4 + n_alias
    o_ref, convout_ref, sout_ref = refs[pos:pos + 3]
    full_ref, qn_ref, kn_ref, v_ref = refs[pos + 3:]
    n = pl.program_id(1)
    H = DN_HEAD
    NH = DN_V_HEADS
    per = DN_V_HEADS // DN_QK_HEADS

    @pl.when(n == 0)
    def _():
        if has_init:
            full_ref[5:8, :] = conv0_ref[0]
            sout_ref[...] = s0_ref[...]
        else:
            full_ref[5:8, :] = jnp.zeros((DN_CONV - 1, DN_CONV_DIM), f32)
            sout_ref[...] = jnp.zeros_like(sout_ref)

    full_ref[8:8 + C, :] = proj_ref[0, :, 0:DN_CONV_DIM]

    for cb in range(DN_CONV_DIM // H):
        cols = slice(cb * H, (cb + 1) * H)
        w = cw_ref[:, cols]
        y = (full_ref[5:5 + C, cols] * w[0:1] + full_ref[6:6 + C, cols] * w[1:2]
             + full_ref[7:7 + C, cols] * w[2:3] + full_ref[8:8 + C, cols] * w[3:4])
        y = y * _sigmoid(y)
        if cb < 2 * DN_QK_HEADS:
            inv = lax.rsqrt(jnp.sum(y * y, axis=-1, keepdims=True) + L2_EPS)
            if cb < DN_QK_HEADS:
                qn_ref[cb] = y * (inv * (H ** -0.5))
            else:
                kn_ref[cb - DN_QK_HEADS] = y * inv
        else:
            v_ref[cb - 2 * DN_QK_HEADS] = y

    tail = full_ref[C + 5:C + 8, :]
    full_ref[5:8, :] = tail

    @pl.when(n == pl.num_programs(1) - 1)
    def _():
        convout_ref[0] = tail

    ba = proj_ref[0, :, DN_CONV_DIM + DN_VAL_DIM:DN_IN_PAD]
    beta = _sigmoid(ba[:, 0:NH])
    xa = ba[:, NH:2 * NH] + dtb_ref[...]
    softplus = jnp.maximum(xa, 0.0) + jnp.log(1.0 + jnp.exp(-jnp.abs(xa)))
    g = -jnp.exp(alog_ref[...]) * softplus

    ri = lax.broadcasted_iota(jnp.int32, (C, C), 0)
    ci = lax.broadcasted_iota(jnp.int32, (C, C), 1)
    causal = ri >= ci
    strict = ri > ci
    eye = jnp.where(ri == ci, 1.0, 0.0).astype(f32)
    tri16 = jnp.where(causal, 1.0, 0.0).astype(bf16)
    g3 = jnp.concatenate(_split3(g), axis=1)
    G3 = _dot(tri16, g3)
    G = G3[:, 0:NH] + G3[:, NH:2 * NH] + G3[:, 2 * NH:3 * NH]
    GT3 = lax.dot_general(g3, tri16, (((0,), (1,)), ((), ())), preferred_element_type=f32)
    GT = GT3[0:NH] + GT3[NH:2 * NH] + GT3[2 * NH:3 * NH]

    ks = [kn_ref[qh] for qh in range(DN_QK_HEADS)]
    qs = [qn_ref[qh] for qh in range(DN_QK_HEADS)]
    kqs = [_dot_nt(jnp.concatenate([ks[qh], qs[qh]], axis=0).astype(bf16), ks[qh].astype(bf16))
           for qh in range(DN_QK_HEADS)]
    kTs = [ks[qh].T for qh in range(DN_QK_HEADS)]
    Gc = [G[:, h:h + 1] for h in range(NH)]
    Gr = [GT[h:h + 1, :] for h in range(NH)]
    bc = [beta[:, h:h + 1] for h in range(NH)]
    decay = [jnp.where(causal, jnp.exp(jnp.where(causal, Gc[h] - Gr[h], 0.0)), 0.0) for h in range(NH)]
    M = [jnp.where(strict, -(bc[h] * kqs[h // per][:C] * decay[h]), 0.0) for h in range(NH)]
    T = [eye + M[h] for h in range(NH)]
    Mp = [_dot(M[h].astype(bf16), M[h].astype(bf16)) for h in range(NH)]
    for it in range(n_iter):
        if it < n_iter - 1:
            tm = [_dot(jnp.concatenate([T[h], Mp[h]], axis=0).astype(bf16), Mp[h].astype(bf16)) for h in range(NH)]
            T = [T[h] + tm[h][:C] for h in range(NH)]
            Mp = [tm[h][C:] for h in range(NH)]
        else:
            tm = [_dot(T[h].astype(bf16), Mp[h].astype(bf16)) for h in range(NH)]
            T = [T[h] + tm[h] for h in range(NH)]
    eG = [jnp.exp(Gc[h]) for h in range(NH)]
    uw = [_dot(T[h].astype(bf16),
               jnp.concatenate([v_ref[h] * bc[h], ks[h // per] * (bc[h] * eG[h])], axis=1).astype(bf16))
          for h in range(NH)]
    S = [sout_ref[0, h] for h in range(NH)]
    ws = [_dot(jnp.concatenate([uw[h][:, H:], qs[h // per] * eG[h]], axis=0).astype(bf16), S[h].astype(bf16))
          for h in range(NH)]
    unew = [uw[h][:, :H] - ws[h][:C] for h in range(NH)]
    Gl = [Gc[h][C - 1:C, :] for h in range(NH)]
    r = [_dot(jnp.concatenate([kqs[h // per][C:] * decay[h], kTs[h // per] * jnp.exp(Gl[h] - Gr[h])],
                              axis=0).astype(bf16), unew[h].astype(bf16))
         for h in range(NH)]
    for h in range(NH):
        sout_ref[0, h] = S[h] * jnp.exp(Gl[h]) + r[h][C:]
        on = _rms(ws[h][C:] + r[h][:C], nw_ref[...])
        z = proj_ref[0, :, DN_CONV_DIM + h * H:DN_CONV_DIM + (h + 1) * H]
        o_ref[0, :, h * H:(h + 1) * H] = (on * (z * _sigmoid(z))).astype(bf16)


def delta_mixer(proj, init, prev_out, layer, n_layers, conv_w, a_log, dt_bias, norm_w, C):
    B, L, _ = proj.shape
    n_iter = max(1, (C - 1).bit_length()) - 1
    H = DN_HEAD
    has_init = init is not None
    n_alias = 0 if prev_out is None else 2
    conv_spec = pl.BlockSpec((None, 1, DN_CONV - 1, DN_CONV_DIM), lambda b, n: (layer, b, 0, 0))
    state_spec = pl.BlockSpec((None, 1, DN_V_HEADS, H, H), lambda b, n: (layer, b, 0, 0, 0))
    small = lambda shape: pl.BlockSpec(shape, lambda b, n: (0,) * len(shape))
    in_specs = [pl.BlockSpec((1, C, DN_IN_PAD), lambda b, n: (b, n, 0))]
    args = [proj]
    if has_init:
        in_specs += [conv_spec, state_spec]
        args += list(init)
    in_specs += [small((DN_CONV, DN_CONV_DIM)), small((1, DN_V_HEADS)), small((1, DN_V_HEADS)), small((1, H))]
    args += [conv_w, a_log.reshape(1, -1), dt_bias.reshape(1, -1), norm_w.reshape(1, -1)]
    aliases = {}
    if prev_out is not None:
        aliases = {len(args): 1, len(args) + 1: 2}
        in_specs += [pl.BlockSpec(memory_space=pl.ANY), pl.BlockSpec(memory_space=pl.ANY)]
        args += list(prev_out)
    return pl.pallas_call(
        functools.partial(_delta_kernel, C, n_iter, has_init, n_alias),
        grid=(B, L // C),
        in_specs=in_specs,
        out_specs=[
            pl.BlockSpec((1, C, DN_VAL_DIM), lambda b, n: (b, n, 0)),
            conv_spec,
            state_spec,
        ],
        out_shape=[
            jax.ShapeDtypeStruct((B, L, DN_VAL_DIM), bf16),
            jax.ShapeDtypeStruct((n_layers, B, DN_CONV - 1, DN_CONV_DIM), f32),
            jax.ShapeDtypeStruct((n_layers, B, DN_V_HEADS, H, H), f32),
        ],
        scratch_shapes=[
            pltpu.VMEM((C + 8, DN_CONV_DIM), f32),
            pltpu.VMEM((DN_QK_HEADS, C, H), f32),
            pltpu.VMEM((DN_QK_HEADS, C, H), f32),
            pltpu.VMEM((DN_V_HEADS, C, H), f32),
        ],
        input_output_aliases=aliases,
        compiler_params=_params("parallel", "arbitrary"),
        name="delta_mixer",
    )(*args)


def _swa_prompt_kernel(sinks_ref, q_ref, kc_ref, kp_ref, vc_ref, vp_ref, o_ref):
    n = pl.program_id(1)
    W = WINDOW
    hd = SWA_HEAD_DIM
    ri = lax.broadcasted_iota(jnp.int32, (W, 2 * W), 0)
    ci = lax.broadcasted_iota(jnp.int32, (W, 2 * W), 1)
    rel = W + ri - ci
    first_col = jnp.where(n > 0, 0, W)
    valid = (rel >= 0) & (rel <= WINDOW) & (ci >= first_col)
    relf = rel.astype(f32)
    k16, v16 = [], []
    for kv in range(SWA_KV_HEADS):
        cs = slice(kv * hd, (kv + 1) * hd)
        k16.append(jnp.concatenate([kp_ref[0, :, cs], kc_ref[0, :, cs]], axis=0).astype(bf16))
        v16.append(jnp.concatenate([vp_ref[0, :, cs], vc_ref[0, :, cs]], axis=0).astype(bf16))
    heads = range(SWA_HEADS)
    s = [_dot_nt(q_ref[0, :, h * hd:(h + 1) * hd].astype(bf16), k16[h // SWA_GROUPS]) for h in heads]
    p = []
    for h in heads:
        sh = jnp.where(valid, s[h] * SWA_SCALE - ALIBI_SLOPES[h] * relf, -jnp.inf)
        sk = sinks_ref[h]
        m = jnp.maximum(jnp.max(sh, axis=-1, keepdims=True), sk)
        e = jnp.exp(sh - m)
        denom = jnp.sum(e, axis=-1, keepdims=True) + jnp.exp(sk - m)
        p.append((e / denom).astype(bf16))
    o = [_dot(p[h], v16[h // SWA_GROUPS]) for h in heads]
    for h in heads:
        o_ref[0, :, h * hd:(h + 1) * hd] = o[h].astype(bf16)


def swa_prompt_attention(qkv, sinks):
    B, L, _ = qkv.shape
    W = WINDOW
    kb = SWA_Q_DIM // SWA_KV_DIM
    prev = lambda n: jnp.maximum(n - 1, 0)
    return pl.pallas_call(
        _swa_prompt_kernel,
        grid=(B, L // W),
        in_specs=[
            pl.BlockSpec(memory_space=pltpu.SMEM),
            pl.BlockSpec((1, W, SWA_Q_DIM), lambda b, n: (b, n, 0)),
            pl.BlockSpec((1, W, SWA_KV_DIM), lambda b, n: (b, n, kb)),
            pl.BlockSpec((1, W, SWA_KV_DIM), lambda b, n: (b, prev(n), kb)),
            pl.BlockSpec((1, W, SWA_KV_DIM), lambda b, n: (b, n, kb + 1)),
            pl.BlockSpec((1, W, SWA_KV_DIM), lambda b, n: (b, prev(n), kb + 1)),
        ],
        out_specs=pl.BlockSpec((1, W, SWA_Q_DIM), lambda b, n: (b, n, 0)),
        out_shape=jax.ShapeDtypeStruct((B, L, SWA_Q_DIM), bf16),
        compiler_params=_params("parallel", "arbitrary"),
        name="swa_prompt",
    )(sinks, qkv, qkv, qkv, qkv, qkv)


def _swa_sample_kernel(NS, L, R, sinks_ref, qkv_ref, ck_ref, cv_ref, o_ref):
    hd = SWA_HEAD_DIM
    G = SWA_GROUPS
    rows = G * L
    t_c = lax.broadcasted_iota(jnp.int32, (rows, R), 0) % L
    j_c = lax.broadcasted_iota(jnp.int32, (rows, R), 1)
    rel_c = R + t_c - j_c
    valid_c = (rel_c >= 0) & (rel_c <= WINDOW)
    t_n = lax.broadcasted_iota(jnp.int32, (rows, L), 0) % L
    j_n = lax.broadcasted_iota(jnp.int32, (rows, L), 1)
    rel_n = t_n - j_n
    valid_n = (rel_n >= 0) & (rel_n <= WINDOW)
    g_row = lax.broadcasted_iota(jnp.int32, (rows, 1), 0) // L
    bias_c, bias_n, sks = [], [], []
    for kv in range(SWA_KV_HEADS):
        slope = jnp.zeros((rows, 1), f32)
        sk = jnp.zeros((rows, 1), f32)
        for gi in range(G):
            h = kv * G + gi
            slope = jnp.where(g_row == gi, ALIBI_SLOPES[h], slope)
            sk = jnp.where(g_row == gi, sinks_ref[h], sk)
        bias_c.append(slope * rel_c.astype(f32))
        bias_n.append(slope * rel_n.astype(f32))
        sks.append(sk)
    pairs = [(b, kv) for b in range(NS) for kv in range(SWA_KV_HEADS)]
    kcol = lambda kv: slice(SWA_Q_DIM + kv * hd, SWA_Q_DIM + (kv + 1) * hd)
    vcol = lambda kv: slice(SWA_Q_DIM + SWA_KV_DIM + kv * hd, SWA_Q_DIM + SWA_KV_DIM + (kv + 1) * hd)
    q16 = [jnp.concatenate([qkv_ref[b, :, (kv * G + gi) * hd:(kv * G + gi + 1) * hd] for gi in range(G)],
                           axis=0).astype(bf16) for b, kv in pairs]
    s_c = [_dot_nt(q16[i], ck_ref[b, :, kv * hd:(kv + 1) * hd].astype(bf16)) for i, (b, kv) in enumerate(pairs)]
    s_n = [_dot_nt(q16[i], qkv_ref[b, :, kcol(kv)].astype(bf16)) for i, (b, kv) in enumerate(pairs)]
    p_c, p_n = [], []
    for i, (b, kv) in enumerate(pairs):
        sc = jnp.where(valid_c, s_c[i] * SWA_SCALE - bias_c[kv], -jnp.inf)
        sn = jnp.where(valid_n, s_n[i] * SWA_SCALE - bias_n[kv], -jnp.inf)
        m = jnp.maximum(jnp.maximum(jnp.max(sc, axis=-1, keepdims=True),
                                    jnp.max(sn, axis=-1, keepdims=True)), sks[kv])
        ec = jnp.exp(sc - m)
        en = jnp.exp(sn - m)
        denom = jnp.sum(ec, axis=-1, keepdims=True) + jnp.sum(en, axis=-1, keepdims=True) + jnp.exp(sks[kv] - m)
        p_c.append((ec / denom).astype(bf16))
        p_n.append((en / denom).astype(bf16))
    o = [_dot(p_c[i], cv_ref[b, :, kv * hd:(kv + 1) * hd].astype(bf16))
         + _dot(p_n[i], qkv_ref[b, :, vcol(kv)].astype(bf16)) for i, (b, kv) in enumerate(pairs)]
    for i, (b, kv) in enumerate(pairs):
        for gi in range(G):
            h = kv * G + gi
            o_ref[b, :, h * hd:(h + 1) * hd] = o[i][gi * L:(gi + 1) * L].astype(bf16)


def swa_sample_attention(qkv, cache_k, cache_v, sinks):
    B, L, _ = qkv.shape
    R = cache_k.shape[1]
    NS = min(B, SWA_SAMPLE_SEQS)
    return pl.pallas_call(
        functools.partial(_swa_sample_kernel, NS, L, R),
        grid=(B // NS,),
        in_specs=[
            pl.BlockSpec(memory_space=pltpu.SMEM),
            pl.BlockSpec((NS, L, SWA_QKV_DIM), lambda b: (b, 0, 0)),
            pl.BlockSpec((NS, R, SWA_KV_DIM), lambda b: (b, 0, 0)),
            pl.BlockSpec((NS, R, SWA_KV_DIM), lambda b: (b, 0, 0)),
        ],
        out_specs=pl.BlockSpec((NS, L, SWA_Q_DIM), lambda b: (b, 0, 0)),
        out_shape=jax.ShapeDtypeStruct((B, L, SWA_Q_DIM), bf16),
        compiler_params=_params("parallel"),
        name="swa_sample",
    )(sinks, qkv, cache_k, cache_v)


def kernel(x_prompt, x_sample, state_conv, state_delta, cache_k, cache_v, norm_mix, norm_ffn, norm_final,
           dn_w_in, dn_conv_w, dn_a_log, dn_dt_bias, dn_norm_w, dn_w_out, swa_w_qkv, swa_b_qkv, swa_sinks,
           swa_w_o, swa_b_o, ffn_w_gu, ffn_w_down):
    Bp, Lp, D = x_prompt.shape
    Bs, Ls, _ = x_sample.shape
    depth = norm_mix.shape[0]
    n_delta = dn_w_in.shape[0]
    yp = x_prompt.reshape(Bp * Lp, D)
    ys = x_sample.reshape(Bs * Ls, D)
    R = cache_k.shape[2]

    w_in = jnp.pad(dn_w_in.astype(bf16), ((0, 0), (0, 0), (0, DN_IN_PAD - DN_IN_DIM)))
    w_out = dn_w_out.astype(bf16)
    w_qkv = swa_w_qkv.astype(bf16)
    w_o = swa_w_o.astype(bf16)
    w_gu = ffn_w_gu.astype(bf16)
    w_down = ffn_w_down.astype(bf16)
    zero_b = jnp.zeros((D,), f32)
    zero_in = jnp.zeros((DN_IN_PAD,), f32)

    kp_list, vp_list, ks_list, vs_list = [], [], [], []
    prompt_states = None
    sample_states = None
    for i in range(depth):
        j = i // 2
        if i % 2 == 0:
            wts = (dn_conv_w[j], dn_a_log[j], dn_dt_bias[j], dn_norm_w[j])
            proj = rms_matmul(yp, norm_mix[i], w_in, j, zero_in, 896).reshape(Bp, Lp, DN_IN_PAD)
            o, cb, sn = delta_mixer(proj, None, prompt_states, j, n_delta, *wts, C=min(DN_CHUNK, Lp))
            prompt_states = (cb, sn)
            yp = matmul_resid(o.reshape(Bp * Lp, DN_VAL_DIM), w_out, j, zero_b, yp)
            proj = rms_matmul(ys, norm_mix[i], w_in, j, zero_in, 896).reshape(Bs, Ls, DN_IN_PAD)
            o, cb, sn = delta_mixer(proj, (state_conv, state_delta), sample_states, j, n_delta, *wts, C=Ls)
            sample_states = (cb, sn)
            ys = matmul_resid(o.reshape(Bs * Ls, DN_VAL_DIM), w_out, j, zero_b, ys)
        else:
            qkv = rms_matmul(yp, norm_mix[i], w_qkv, j, swa_b_qkv[j], 768).reshape(Bp, Lp, SWA_QKV_DIM)
            o = swa_prompt_attention(qkv, swa_sinks[j])
            yp = matmul_resid(o.reshape(Bp * Lp, SWA_Q_DIM), w_o, j, swa_b_o[j], yp)
            Rp = min(WINDOW, Lp)
            kp_list.append(qkv[:, Lp - Rp:, SWA_Q_DIM:SWA_Q_DIM + SWA_KV_DIM].reshape(Bp, Rp, SWA_KV_HEADS, SWA_HEAD_DIM))
            vp_list.append(qkv[:, Lp - Rp:, SWA_Q_DIM + SWA_KV_DIM:].reshape(Bp, Rp, SWA_KV_HEADS, SWA_HEAD_DIM))
            qkv = rms_matmul(ys, norm_mix[i], w_qkv, j, swa_b_qkv[j], 768).reshape(Bs, Ls, SWA_QKV_DIM)
            o = swa_sample_attention(qkv, cache_k[j].reshape(Bs, R, SWA_KV_DIM), cache_v[j].reshape(Bs, R, SWA_KV_DIM),
                                     swa_sinks[j])
            ys = matmul_resid(o.reshape(Bs * Ls, SWA_Q_DIM), w_o, j, swa_b_o[j], ys)
            ks_list.append(qkv[:, :, SWA_Q_DIM:SWA_Q_DIM + SWA_KV_DIM].reshape(Bs, Ls, SWA_KV_HEADS, SWA_HEAD_DIM))
            vs_list.append(qkv[:, :, SWA_Q_DIM + SWA_KV_DIM:].reshape(Bs, Ls, SWA_KV_HEADS, SWA_HEAD_DIM))
        last = i == depth - 1
        yp = ffn(yp, norm_ffn[i], w_gu, w_down, i, norm_final, last)
        ys = ffn(ys, norm_ffn[i], w_gu, w_down, i, norm_final, last)

    return (yp.reshape(Bp, Lp, D), ys.reshape(Bs, Ls, D),
            prompt_states[0], prompt_states[1], jnp.stack(kp_list), jnp.stack(vp_list),
            sample_states[0], sample_states[1], jnp.stack(ks_list), jnp.stack(vs_list))
```

```python
import functools

import jax
import jax.numpy as jnp
from jax import lax
from jax.experimental import pallas as pl
from jax.experimental.pallas import tpu as pltpu

f32 = jnp.float32
bf16 = jnp.bfloat16

D_MODEL = 1024
DN_QK_HEADS = 8
DN_V_HEADS = 16
DN_HEAD = 128
DN_KEY_DIM = DN_QK_HEADS * DN_HEAD
DN_VAL_DIM = DN_V_HEADS * DN_HEAD
DN_CONV_DIM = 2 * DN_KEY_DIM + DN_VAL_DIM
DN_CONV = 4
DN_IN_DIM = DN_CONV_DIM + DN_VAL_DIM + 2 * DN_V_HEADS
DN_IN_PAD = 6272
SWA_HEADS = 16
SWA_KV_HEADS = 4
SWA_GROUPS = SWA_HEADS // SWA_KV_HEADS
SWA_HEAD_DIM = 64
SWA_Q_DIM = SWA_HEADS * SWA_HEAD_DIM
SWA_KV_DIM = SWA_KV_HEADS * SWA_HEAD_DIM
SWA_QKV_DIM = SWA_Q_DIM + 2 * SWA_KV_DIM
SWA_SCALE = SWA_HEAD_DIM ** -0.5
WINDOW = 128
D_FF = 2816
RMS_EPS = 1e-6
L2_EPS = 1e-6
ALIBI_SLOPES = tuple(float(2.0 ** (-8.0 * (i + 1) / SWA_HEADS)) for i in range(SWA_HEADS))

VMEM_LIMIT_BYTES = 56 * 1024 * 1024
DN_CHUNK = 64
SWA_SAMPLE_SEQS = 8


def _params(*sem):
    return pltpu.CompilerParams(dimension_semantics=sem, vmem_limit_bytes=VMEM_LIMIT_BYTES)


def _sigmoid(x):
    return 1.0 / (1.0 + jnp.exp(-x))


def _rms(x, w):
    return x * lax.rsqrt(jnp.mean(x * x, axis=-1, keepdims=True) + RMS_EPS) * w


def _dot(a, b):
    return jnp.dot(a, b, preferred_element_type=f32)


def _dot_nt(a, b):
    return lax.dot_general(a, b, (((1,), (1,)), ((), ())), preferred_element_type=f32)


def _dot_tn(a, b):
    return lax.dot_general(a, b, (((0,), (0,)), ((), ())), preferred_element_type=f32)


def _rms_matmul_kernel(x_ref, nw_ref, w_ref, b_ref, o_ref, xn_ref):
    @pl.when(pl.program_id(1) == 0)
    def _():
        xn_ref[...] = _rms(x_ref[...], nw_ref[...]).astype(bf16)

    o_ref[...] = _dot(xn_ref[...], w_ref[...]) + b_ref[...]


def rms_matmul(x, nw, w_stack, layer, b, tn):
    m, d = x.shape
    n = w_stack.shape[2]
    tm = min(m, 1024)
    return pl.pallas_call(
        _rms_matmul_kernel,
        grid=(m // tm, n // tn),
        in_specs=[
            pl.BlockSpec((tm, d), lambda i, j: (i, 0)),
            pl.BlockSpec((1, d), lambda i, j: (0, 0)),
            pl.BlockSpec((None, d, tn), lambda i, j: (layer, 0, j)),
            pl.BlockSpec((1, tn), lambda i, j: (0, j)),
        ],
        out_specs=pl.BlockSpec((tm, tn), lambda i, j: (i, j)),
        out_shape=jax.ShapeDtypeStruct((m, n), f32),
        scratch_shapes=[pltpu.VMEM((tm, d), bf16)],
        compiler_params=_params("parallel", "arbitrary"),
        name="rms_matmul",
    )(x, nw.reshape(1, d), w_stack, b.reshape(1, n))


def _matmul_resid_kernel(a_ref, w_ref, b_ref, r_ref, o_ref):
    o_ref[...] = r_ref[...] + _dot(a_ref[...], w_ref[...]) + b_ref[...]


def matmul_resid(a, w_stack, layer, b, resid):
    m, k = a.shape
    n = w_stack.shape[2]
    tm = min(m, 512)
    return pl.pallas_call(
        _matmul_resid_kernel,
        grid=(m // tm,),
        in_specs=[
            pl.BlockSpec((tm, k), lambda i: (i, 0)),
            pl.BlockSpec((None, k, n), lambda i: (layer, 0, 0)),
            pl.BlockSpec((1, n), lambda i: (0, 0)),
            pl.BlockSpec((tm, n), lambda i: (i, 0)),
        ],
        out_specs=pl.BlockSpec((tm, n), lambda i: (i, 0)),
        out_shape=jax.ShapeDtypeStruct((m, n), f32),
        compiler_params=_params("parallel"),
        name="matmul_resid",
    )(a, w_stack, b.reshape(1, n), resid)


def _ffn_kernel(final_norm, x_ref, nw_ref, wg_ref, wu_ref, wd_ref, nf_ref, o_ref, xn_ref, acc_ref):
    j = pl.program_id(1)

    @pl.when(j == 0)
    def _():
        xn_ref[...] = _rms(x_ref[...], nw_ref[...]).astype(bf16)
        acc_ref[...] = jnp.zeros_like(acc_ref)

    xn = xn_ref[...]
    g = _dot(xn, wg_ref[...])
    u = _dot(xn, wu_ref[...])
    h = (g * _sigmoid(g) * u).astype(bf16)
    acc_ref[...] += _dot(h, wd_ref[...])

    @pl.when(j == pl.num_programs(1) - 1)
    def _():
        y = x_ref[...] + acc_ref[...]
        if final_norm:
            y = _rms(y, nf_ref[...])
        o_ref[...] = y


def ffn(x, nw, wgu_stack, wd_stack, layer, nf, final_norm):
    m, d = x.shape
    tm = min(m, 512)
    tf = D_FF // 2
    nj = D_FF // tf
    return pl.pallas_call(
        functools.partial(_ffn_kernel, final_norm),
        grid=(m // tm, nj),
        in_specs=[
            pl.BlockSpec((tm, d), lambda i, j: (i, 0)),
            pl.BlockSpec((1, d), lambda i, j: (0, 0)),
            pl.BlockSpec((None, d, tf), lambda i, j: (layer, 0, j)),
            pl.BlockSpec((None, d, tf), lambda i, j: (layer, 0, j + nj)),
            pl.BlockSpec((None, tf, d), lambda i, j: (layer, j, 0)),
            pl.BlockSpec((1, d), lambda i, j: (0, 0)),
        ],
        out_specs=pl.BlockSpec((tm, d), lambda i, j: (i, 0)),
        out_shape=jax.ShapeDtypeStruct((m, d), f32),
        scratch_shapes=[pltpu.VMEM((tm, d), bf16), pltpu.VMEM((tm, d), f32)],
        compiler_params=_params("parallel", "arbitrary"),
        name="ffn",
    )(x, nw.reshape(1, d), wgu_stack, wgu_stack, wd_stack, nf.reshape(1, d))


def _split3(x):
    hi = x.astype(bf16)
    r1 = x - hi.astype(f32)
    mid = r1.astype(bf16)
    lo = (r1 - mid.astype(f32)).astype(bf16)
    return hi, mid, lo


def _delta_kernel(C, n_iter, has_init, n_alias, *refs):
    proj_ref = refs[0]
    pos = 1
    if has_init:
        conv0_ref, s0_ref = refs[1:3]
        pos = 3
    cw_ref, alog_ref, dtb_ref, nw_ref = refs[pos:pos + 4]
    pos += 4 + n_alias
    o_ref, convout_ref, sout_ref = refs[pos:pos + 3]
    full_ref, qn_ref, kn_ref = refs[pos + 3:]
    n = pl.program_id(1)
    H = DN_HEAD
    NH = DN_V_HEADS
    per = DN_V_HEADS // DN_QK_HEADS

    @pl.when(n == 0)
    def _():
        if has_init:
            full_ref[5:8, :] = conv0_ref[0]
            sout_ref[...] = s0_ref[...]
        else:
            full_ref[5:8, :] = jnp.zeros((DN_CONV - 1, DN_CONV_DIM), f32)
            sout_ref[...] = jnp.zeros_like(sout_ref)

    full_ref[8:8 + C, :] = proj_ref[0, :, 0:DN_CONV_DIM]

    def conv_silu(cb):
        cols = slice(cb * H, (cb + 1) * H)
        w = cw_ref[:, cols]
        y = (full_ref[5:5 + C, cols] * w[0:1] + full_ref[6:6 + C, cols] * w[1:2]
             + full_ref[7:7 + C, cols] * w[2:3] + full_ref[8:8 + C, cols] * w[3:4])
        return y * _sigmoid(y)

    for cb in range(2 * DN_QK_HEADS):
        y = conv_silu(cb)
        inv = lax.rsqrt(jnp.sum(y * y, axis=-1, keepdims=True) + L2_EPS)
        if cb < DN_QK_HEADS:
            qn_ref[cb] = y * (inv * (H ** -0.5))
        else:
            kn_ref[cb - DN_QK_HEADS] = y * inv

    ba = proj_ref[0, :, DN_CONV_DIM + DN_VAL_DIM:DN_IN_PAD]
    beta = _sigmoid(ba[:, 0:NH])
    xa = ba[:, NH:2 * NH] + dtb_ref[...]
    softplus = jnp.maximum(xa, 0.0) + jnp.log(1.0 + jnp.exp(-jnp.abs(xa)))
    g = -jnp.exp(alog_ref[...]) * softplus

    ri = lax.broadcasted_iota(jnp.int32, (C, C), 0)
    ci = lax.broadcasted_iota(jnp.int32, (C, C), 1)
    causal = ri >= ci
    strict = ri > ci
    eye = jnp.where(ri == ci, 1.0, 0.0).astype(f32)
    tri16 = jnp.where(causal, 1.0, 0.0).astype(bf16)
    g3 = jnp.concatenate(_split3(g), axis=1)
    G3 = _dot(tri16, g3)
    G = G3[:, 0:NH] + G3[:, NH:2 * NH] + G3[:, 2 * NH:3 * NH]
    GT3 = lax.dot_general(g3, tri16, (((0,), (1,)), ((), ())), preferred_element_type=f32)
    GT = GT3[0:NH] + GT3[NH:2 * NH] + GT3[2 * NH:3 * NH]

    ks = [kn_ref[qh] for qh in range(DN_QK_HEADS)]
    qs = [qn_ref[qh] for qh in range(DN_QK_HEADS)]
    kqs = [_dot_nt(jnp.concatenate([ks[qh], qs[qh]], axis=0).astype(bf16), ks[qh].astype(bf16))
           for qh in range(DN_QK_HEADS)]
    kTs = [ks[qh].T for qh in range(DN_QK_HEADS)]
    Gc = [G[:, h:h + 1] for h in range(NH)]
    Gr = [GT[h:h + 1, :] for h in range(NH)]
    bc = [beta[:, h:h + 1] for h in range(NH)]
    decay = [jnp.where(causal, jnp.exp(jnp.where(causal, Gc[h] - Gr[h], 0.0)), 0.0) for h in range(NH)]
    M = [jnp.where(strict, -(bc[h] * kqs[h // per][:C] * decay[h]), 0.0) for h in range(NH)]
    T = [eye + M[h] for h in range(NH)]
    Mp = [_dot(M[h].astype(bf16), M[h].astype(bf16)) for h in range(NH)]
    for it in range(n_iter):
        if it < n_iter - 1:
            tm = [_dot(jnp.concatenate([T[h], Mp[h]], axis=0).astype(bf16), Mp[h].astype(bf16)) for h in range(NH)]
            T = [T[h] + tm[h][:C] for h in range(NH)]
            Mp = [tm[h][C:] for h in range(NH)]
        else:
            tm = [_dot(T[h].astype(bf16), Mp[h].astype(bf16)) for h in range(NH)]
            T = [T[h] + tm[h] for h in range(NH)]
    vs = [conv_silu(2 * DN_QK_HEADS + h) for h in range(NH)]
    tail = full_ref[C + 5:C + 8, :]
    full_ref[5:8, :] = tail
    convout_ref[0] = tail

    eG =[jnp.exp(Gc[h]) for h in range(NH)]
    uw = [_dot(T[h].astype(bf16),
               jnp.concatenate([vs[h] * bc[h], ks[h // per] * (bc[h] * eG[h])], axis=1).astype(bf16))
          for h in range(NH)]
    S = [sout_ref[0, h] for h in range(NH)]
    ws = [_dot(jnp.concatenate([uw[h][:, H:], qs[h // per] * eG[h]], axis=0).astype(bf16), S[h].astype(bf16))
          for h in range(NH)]
    unew = [uw[h][:, :H] - ws[h][:C] for h in range(NH)]
    Gl = [Gc[h][C - 1:C, :] for h in range(NH)]
    r = [_dot(jnp.concatenate([kqs[h // per][C:] * decay[h], kTs[h // per] * jnp.exp(Gl[h] - Gr[h])],
                              axis=0).astype(bf16), unew[h].astype(bf16))
         for h in range(NH)]
    for h in range(NH):
        sout_ref[0, h] = S[h] * jnp.exp(Gl[h]) + r[h][C:]
        on = _rms(ws[h][C:] + r[h][:C], nw_ref[...])
        z = proj_ref[0, :, DN_CONV_DIM + h * H:DN_CONV_DIM + (h + 1) * H]
        o_ref[0, :, h * H:(h + 1) * H] = (on * (z * _sigmoid(z))).astype(bf16)


def delta_mixer(proj, init, prev_out, layer, n_layers, conv_w, a_log, dt_bias, norm_w, C):
    B, L, _ = proj.shape
    n_iter = max(1, (C - 1).bit_length()) - 1
    H = DN_HEAD
    has_init = init is not None
    n_alias = 0 if prev_out is None else 2
    conv_spec = pl.BlockSpec((None, 1, DN_CONV - 1, DN_CONV_DIM), lambda b, n: (layer, b, 0, 0))
    state_spec = pl.BlockSpec((None, 1, DN_V_HEADS, H, H), lambda b, n: (layer, b, 0, 0, 0))
    small = lambda shape: pl.BlockSpec(shape, lambda b, n: (0,) * len(shape))
    in_specs = [pl.BlockSpec((1, C, DN_IN_PAD), lambda b, n: (b, n, 0))]
    args = [proj]
    if has_init:
        in_specs += [conv_spec, state_spec]
        args += list(init)
    in_specs += [small((DN_CONV, DN_CONV_DIM)), small((1, DN_V_HEADS)), small((1, DN_V_HEADS)), small((1, H))]
    args += [conv_w, a_log.reshape(1, -1), dt_bias.reshape(1, -1), norm_w.reshape(1, -1)]
    aliases = {}
    if prev_out is not None:
        aliases = {len(args): 1, len(args) + 1: 2}
        in_specs += [pl.BlockSpec(memory_space=pl.ANY), pl.BlockSpec(memory_space=pl.ANY)]
        args += list(prev_out)
    return pl.pallas_call(
        functools.partial(_delta_kernel, C, n_iter, has_init, n_alias),
        grid=(B, L // C),
        in_specs=in_specs,
        out_specs=[
            pl.BlockSpec((1, C, DN_VAL_DIM), lambda b, n: (b, n, 0)),
            conv_spec,
            state_spec,
        ],
        out_shape=[
            jax.ShapeDtypeStruct((B, L, DN_VAL_DIM), bf16),
            jax.ShapeDtypeStruct((n_layers, B, DN_CONV - 1, DN_CONV_DIM), f32),
            jax.ShapeDtypeStruct((n_layers, B, DN_V_HEADS, H, H), f32),
        ],
        scratch_shapes=[
            pltpu.VMEM((C + 8, DN_CONV_DIM), f32),
            pltpu.VMEM((DN_QK_HEADS, C, H), f32),
            pltpu.VMEM((DN_QK_HEADS, C, H), f32),
        ],
        input_output_aliases=aliases,
        compiler_params=_params("parallel", "arbitrary"),
        name="delta_mixer",
    )(*args)


def _swa_prompt_kernel(sinks_ref, q_ref, kc_ref, kp_ref, vc_ref, vp_ref, o_ref):
    n = pl.program_id(1)
    W = WINDOW
    hd = SWA_HEAD_DIM
    ri = lax.broadcasted_iota(jnp.int32, (W, 2 * W), 0)
    ci = lax.broadcasted_iota(jnp.int32, (W, 2 * W), 1)
    rel = W + ri - ci
    first_col = jnp.where(n > 0, 0, W)
    valid = (rel >= 0) & (rel <= WINDOW) & (ci >= first_col)
    relf = rel.astype(f32)
    k16, v16 = [], []
    for kv in range(SWA_KV_HEADS):
        cs = slice(kv * hd, (kv + 1) * hd)
        k16.append(jnp.concatenate([kp_ref[0, :, cs], kc_ref[0, :, cs]], axis=0).astype(bf16))
        v16.append(jnp.concatenate([vp_ref[0, :, cs], vc_ref[0, :, cs]], axis=0).astype(bf16))
    heads = range(SWA_HEADS)
    s = [_dot_nt(q_ref[0, :, h * hd:(h + 1) * hd].astype(bf16), k16[h // SWA_GROUPS]) for h in heads]
    p = []
    for h in heads:
        sh = jnp.where(valid, s[h] * SWA_SCALE - ALIBI_SLOPES[h] * relf, -jnp.inf)
        sk = sinks_ref[h]
        m = jnp.maximum(jnp.max(sh, axis=-1, keepdims=True), sk)
        e = jnp.exp(sh - m)
        denom = jnp.sum(e, axis=-1, keepdims=True) + jnp.exp(sk - m)
        p.append((e / denom).astype(bf16))
    o = [_dot(p[h], v16[h // SWA_GROUPS]) for h in heads]
    for h in heads:
        o_ref[0, :, h * hd:(h + 1) * hd] = o[h].astype(bf16)


def swa_prompt_attention(qkv, sinks):
    B, L, _ = qkv.shape
    W = WINDOW
    kb = SWA_Q_DIM // SWA_KV_DIM
    prev = lambda n: jnp.maximum(n - 1, 0)
    return pl.pallas_call(
        _swa_prompt_kernel,
        grid=(B, L // W),
        in_specs=[
            pl.BlockSpec(memory_space=pltpu.SMEM),
            pl.BlockSpec((1, W, SWA_Q_DIM), lambda b, n: (b, n, 0)),
            pl.BlockSpec((1, W, SWA_KV_DIM), lambda b, n: (b, n, kb)),
            pl.BlockSpec((1, W, SWA_KV_DIM), lambda b, n: (b, prev(n), kb)),
            pl.BlockSpec((1, W, SWA_KV_DIM), lambda b, n: (b, n, kb + 1)),
            pl.BlockSpec((1, W, SWA_KV_DIM), lambda b, n: (b, prev(n), kb + 1)),
        ],
        out_specs=pl.BlockSpec((1, W, SWA_Q_DIM), lambda b, n: (b, n, 0)),
        out_shape=jax.ShapeDtypeStruct((B, L, SWA_Q_DIM), bf16),
        compiler_params=_params("parallel", "arbitrary"),
        name="swa_prompt",
    )(sinks, qkv, qkv, qkv, qkv, qkv)


def _swa_sample_kernel(NS, L, R, sinks_ref, qkv_ref, ck_ref, cv_ref, o_ref):
    hd = SWA_HEAD_DIM
    G = SWA_GROUPS
    rows = G * L
    t_c = lax.broadcasted_iota(jnp.int32, (rows, R), 0) % L
    j_c = lax.broadcasted_iota(jnp.int32, (rows, R), 1)
    rel_c = R + t_c - j_c
    valid_c = (rel_c >= 0) & (rel_c <= WINDOW)
    t_n = lax.broadcasted_iota(jnp.int32, (rows, L), 0) % L
    j_n = lax.broadcasted_iota(jnp.int32, (rows, L), 1)
    rel_n = t_n - j_n
    valid_n = (rel_n >= 0) & (rel_n <= WINDOW)
    g_row = lax.broadcasted_iota(jnp.int32, (rows, 1), 0) // L
    bias_c, bias_n, sks = [], [], []
    for kv in range(SWA_KV_HEADS):
        slope = jnp.zeros((rows, 1), f32)
        sk = jnp.zeros((rows, 1), f32)
        for gi in range(G):
            h = kv * G + gi
            slope = jnp.where(g_row == gi, ALIBI_SLOPES[h], slope)
            sk = jnp.where(g_row == gi, sinks_ref[h], sk)
        bias_c.append(slope * rel_c.astype(f32))
        bias_n.append(slope * rel_n.astype(f32))
        sks.append(sk)
    pairs = [(b, kv) for b in range(NS) for kv in range(SWA_KV_HEADS)]
    kcol = lambda kv: slice(SWA_Q_DIM + kv * hd, SWA_Q_DIM + (kv + 1) * hd)
    vcol = lambda kv: slice(SWA_Q_DIM + SWA_KV_DIM + kv * hd, SWA_Q_DIM + SWA_KV_DIM + (kv + 1) * hd)
    q16 = [jnp.concatenate([qkv_ref[b, :, (kv * G + gi) * hd:(kv * G + gi + 1) * hd] for gi in range(G)],
                           axis=0).astype(bf16) for b, kv in pairs]
    s_c = [_dot_nt(q16[i], ck_ref[b, :, kv * hd:(kv + 1) * hd].astype(bf16)) for i, (b, kv) in enumerate(pairs)]
    s_n = [_dot_nt(q16[i], qkv_ref[b, :, kcol(kv)].astype(bf16)) for i, (b, kv) in enumerate(pairs)]
    p_c, p_n = [], []
    for i, (b, kv) in enumerate(pairs):
        sc = jnp.where(valid_c, s_c[i] * SWA_SCALE - bias_c[kv], -jnp.inf)
        sn = jnp.where(valid_n, s_n[i] * SWA_SCALE - bias_n[kv], -jnp.inf)
        m = jnp.maximum(jnp.maximum(jnp.max(sc, axis=-1, keepdims=True),
                                    jnp.max(sn, axis=-1, keepdims=True)), sks[kv])
        ec = jnp.exp(sc - m)
        en = jnp.exp(sn - m)
        denom = jnp.sum(ec, axis=-1, keepdims=True) + jnp.sum(en, axis=-1, keepdims=True) + jnp.exp(sks[kv] - m)
        p_c.append((ec / denom).astype(bf16))
        p_n.append((en / denom).astype(bf16))
    o = [_dot(p_c[i], cv_ref[b, :, kv * hd:(kv + 1) * hd].astype(bf16))
         + _dot(p_n[i], qkv_ref[b, :, vcol(kv)].astype(bf16)) for i, (b, kv) in enumerate(pairs)]
    for i, (b, kv) in enumerate(pairs):
        for gi in range(G):
            h = kv * G + gi
            o_ref[b, :, h * hd:(h + 1) * hd] = o[i][gi * L:(gi + 1) * L].astype(bf16)


def swa_sample_attention(qkv, cache_k, cache_v, sinks):
    B, L, _ = qkv.shape
    R = cache_k.shape[1]
    NS = min(B, SWA_SAMPLE_SEQS)
    return pl.pallas_call(
        functools.partial(_swa_sample_kernel, NS, L, R),
        grid=(B // NS,),
        in_specs=[
            pl.BlockSpec(memory_space=pltpu.SMEM),
            pl.BlockSpec((NS, L, SWA_QKV_DIM), lambda b: (b, 0, 0)),
            pl.BlockSpec((NS, R, SWA_KV_DIM), lambda b: (b, 0, 0)),
            pl.BlockSpec((NS, R, SWA_KV_DIM), lambda b: (b, 0, 0)),
        ],
        out_specs=pl.BlockSpec((NS, L, SWA_Q_DIM), lambda b: (b, 0, 0)),
        out_shape=jax.ShapeDtypeStruct((B, L, SWA_Q_DIM), bf16),
        compiler_params=_params("parallel"),
        name="swa_sample",
    )(sinks, qkv, cache_k, cache_v)


def kernel(x_prompt, x_sample, state_conv, state_delta, cache_k, cache_v, norm_mix, norm_ffn, norm_final,
           dn_w_in, dn_conv_w, dn_a_log, dn_dt_bias, dn_norm_w, dn_w_out, swa_w_qkv, swa_b_qkv, swa_sinks,
           swa_w_o, swa_b_o, ffn_w_gu, ffn_w_down):
    Bp, Lp, D = x_prompt.shape
    Bs, Ls, _ = x_sample.shape
    depth = norm_mix.shape[0]
    n_delta = dn_w_in.shape[0]
    yp = x_prompt.reshape(Bp * Lp, D)
    ys = x_sample.reshape(Bs * Ls, D)
    R = cache_k.shape[2]

    w_in = jnp.pad(dn_w_in.astype(bf16), ((0, 0), (0, 0), (0, DN_IN_PAD - DN_IN_DIM)))
    w_out = dn_w_out.astype(bf16)
    w_qkv = swa_w_qkv.astype(bf16)
    w_o = swa_w_o.astype(bf16)
    w_gu = ffn_w_gu.astype(bf16)
    w_down = ffn_w_down.astype(bf16)
    zero_b = jnp.zeros((D,), f32)
    zero_in = jnp.zeros((DN_IN_PAD,), f32)

    kp_list, vp_list, ks_list, vs_list = [], [], [], []
    prompt_states = None
    sample_states = None
    for i in range(depth):
        j = i // 2
        if i % 2 == 0:
            wts = (dn_conv_w[j], dn_a_log[j], dn_dt_bias[j], dn_norm_w[j])
            proj = rms_matmul(yp, norm_mix[i], w_in, j, zero_in, 896).reshape(Bp, Lp, DN_IN_PAD)
            o, cb, sn = delta_mixer(proj, None, prompt_states, j, n_delta, *wts, C=min(DN_CHUNK, Lp))
            prompt_states = (cb, sn)
            yp = matmul_resid(o.reshape(Bp * Lp, DN_VAL_DIM), w_out, j, zero_b, yp)
            proj = rms_matmul(ys, norm_mix[i], w_in, j, zero_in, 896).reshape(Bs, Ls, DN_IN_PAD)
            o, cb, sn = delta_mixer(proj, (state_conv, state_delta), sample_states, j, n_delta, *wts, C=Ls)
            sample_states = (cb, sn)
            ys = matmul_resid(o.reshape(Bs * Ls, DN_VAL_DIM), w_out, j, zero_b, ys)
        else:
            qkv = rms_matmul(yp, norm_mix[i], w_qkv, j, swa_b_qkv[j], 768).reshape(Bp, Lp, SWA_QKV_DIM)
            o = swa_prompt_attention(qkv, swa_sinks[j])
            yp = matmul_resid(o.reshape(Bp * Lp, SWA_Q_DIM), w_o, j, swa_b_o[j], yp)
            Rp = min(WINDOW, Lp)
            kp_list.append(qkv[:, Lp - Rp:, SWA_Q_DIM:SWA_Q_DIM + SWA_KV_DIM].reshape(Bp, Rp, SWA_KV_HEADS, SWA_HEAD_DIM))
            vp_list.append(qkv[:, Lp - Rp:, SWA_Q_DIM + SWA_KV_DIM:].reshape(Bp, Rp, SWA_KV_HEADS, SWA_HEAD_DIM))
            qkv = rms_matmul(ys, norm_mix[i], w_qkv, j, swa_b_qkv[j], 768).reshape(Bs, Ls, SWA_QKV_DIM)
            o = swa_sample_attention(qkv, cache_k[j].reshape(Bs, R, SWA_KV_DIM), cache_v[j].reshape(Bs, R, SWA_KV_DIM),
                                     swa_sinks[j])
            ys = matmul_resid(o.reshape(Bs * Ls, SWA_Q_DIM), w_o, j, swa_b_o[j], ys)
            ks_list.append(qkv[:, :, SWA_Q_DIM:SWA_Q_DIM + SWA_KV_DIM].reshape(Bs, Ls, SWA_KV_HEADS, SWA_HEAD_DIM))
            vs_list.append(qkv[:, :, SWA_Q_DIM + SWA_KV_DIM:].reshape(Bs, Ls, SWA_KV_HEADS, SWA_HEAD_DIM))
        last = i == depth - 1
        yp = ffn(yp, norm_ffn[i], w_gu, w_down, i, norm_final, last)
        ys = ffn(ys, norm_ffn[i], w_gu, w_down, i, norm_final, last)

    return (yp.reshape(Bp, Lp, D), ys.reshape(Bs, Ls, D),
            prompt_states[0], prompt_states[1], jnp.stack(kp_list), jnp.stack(vp_list),
            sample_states[0], sample_states[1], jnp.stack(ks_list), jnp.stack(vs_list))
```

```python
import functools

import jax
import jax.numpy as jnp
from jax import lax
from jax.experimental import pallas as pl
from jax.experimental.pallas import tpu as pltpu

f32 = jnp.float32
bf16 = jnp.bfloat16

D_MODEL = 1024
DN_QK_HEADS = 8
DN_V_HEADS = 16
DN_HEAD = 128
DN_KEY_DIM = DN_QK_HEADS * DN_HEAD
DN_VAL_DIM = DN_V_HEADS * DN_HEAD
DN_CONV_DIM = 2 * DN_KEY_DIM + DN_VAL_DIM
DN_CONV = 4
DN_IN_DIM = DN_CONV_DIM + DN_VAL_DIM + 2 * DN_V_HEADS
DN_IN_PAD = 6272
SWA_HEADS = 16
SWA_KV_HEADS = 4
SWA_GROUPS = SWA_HEADS // SWA_KV_HEADS
SWA_HEAD_DIM = 64
SWA_Q_DIM = SWA_HEADS * SWA_HEAD_DIM
SWA_KV_DIM = SWA_KV_HEADS * SWA_HEAD_DIM
SWA_QKV_DIM = SWA_Q_DIM + 2 * SWA_KV_DIM
SWA_SCALE = SWA_HEAD_DIM ** -0.5
WINDOW = 128
D_FF = 2816
RMS_EPS = 1e-6
L2_EPS = 1e-6
ALIBI_SLOPES = tuple(float(2.0 ** (-8.0 * (i + 1) / SWA_HEADS)) for i in range(SWA_HEADS))

VMEM_LIMIT_BYTES = 56 * 1024 * 1024
DN_CHUNK = 64
SWA_SAMPLE_SEQS = 8
DN_SAMPLE_SEQS = 4


def _params(*sem):
    return pltpu.CompilerParams(dimension_semantics=sem, vmem_limit_bytes=VMEM_LIMIT_BYTES)


def _sigmoid(x):
    return 1.0 / (1.0 + jnp.exp(-x))


def _rms(x, w):
    return x * lax.rsqrt(jnp.mean(x * x, axis=-1, keepdims=True) + RMS_EPS) * w


def _dot(a, b):
    return jnp.dot(a, b, preferred_element_type=f32)


def _dot_nt(a, b):
    return lax.dot_general(a, b, (((1,), (1,)), ((), ())), preferred_element_type=f32)


def _dot_tn(a, b):
    return lax.dot_general(a, b, (((0,), (0,)), ((), ())), preferred_element_type=f32)


def _rms_matmul_kernel(x_ref, nw_ref, w_ref, b_ref, o_ref, xn_ref):
    @pl.when(pl.program_id(1) == 0)
    def _():
        xn_ref[...] = _rms(x_ref[...], nw_ref[...]).astype(bf16)

    o_ref[...] = _dot(xn_ref[...], w_ref[...]) + b_ref[...]


def rms_matmul(x, nw, w_stack, layer, b, tn):
    m, d = x.shape
    n = w_stack.shape[2]
    tm = min(m, 1024)
    return pl.pallas_call(
        _rms_matmul_kernel,
        grid=(m // tm, n // tn),
        in_specs=[
            pl.BlockSpec((tm, d), lambda i, j: (i, 0)),
            pl.BlockSpec((1, d), lambda i, j: (0, 0)),
            pl.BlockSpec((None, d, tn), lambda i, j: (layer, 0, j)),
            pl.BlockSpec((1, tn), lambda i, j: (0, j)),
        ],
        out_specs=pl.BlockSpec((tm, tn), lambda i, j: (i, j)),
        out_shape=jax.ShapeDtypeStruct((m, n), f32),
        scratch_shapes=[pltpu.VMEM((tm, d), bf16)],
        compiler_params=_params("parallel", "arbitrary"),
        name="rms_matmul",
    )(x, nw.reshape(1, d), w_stack, b.reshape(1, n))


def _matmul_resid_kernel(a_ref, w_ref, b_ref, r_ref, o_ref):
    o_ref[...] = r_ref[...] + _dot(a_ref[...], w_ref[...]) + b_ref[...]


def matmul_resid(a, w_stack, layer, b, resid):
    m, k = a.shape
    n = w_stack.shape[2]
    tm = min(m, 1024)
    return pl.pallas_call(
        _matmul_resid_kernel,
        grid=(m // tm,),
        in_specs=[
            pl.BlockSpec((tm, k), lambda i: (i, 0)),
            pl.BlockSpec((None, k, n), lambda i: (layer, 0, 0)),
            pl.BlockSpec((1, n), lambda i: (0, 0)),
            pl.BlockSpec((tm, n), lambda i: (i, 0)),
        ],
        out_specs=pl.BlockSpec((tm, n), lambda i: (i, 0)),
        out_shape=jax.ShapeDtypeStruct((m, n), f32),
        compiler_params=_params("parallel"),
        name="matmul_resid",
    )(a, w_stack, b.reshape(1, n), resid)


def _ffn_kernel(final_norm, x_ref, nw_ref, wg_ref, wu_ref, wd_ref, nf_ref, o_ref, xn_ref, acc_ref):
    j = pl.program_id(1)

    @pl.when(j == 0)
    def _():
        xn_ref[...] = _rms(x_ref[...], nw_ref[...]).astype(bf16)
        acc_ref[...] = jnp.zeros_like(acc_ref)

    xn = xn_ref[...]
    tf = wg_ref.shape[1]
    half = (tf // 256) * 128
    for c0, c1 in ((0, half), (half, tf)):
        g = _dot(xn, wg_ref[:, c0:c1])
        u = _dot(xn, wu_ref[:, c0:c1])
        h = (g * _sigmoid(g) * u).astype(bf16)
        acc_ref[...] += _dot(h, wd_ref[c0:c1, :])

    @pl.when(j == pl.num_programs(1) - 1)
    def _():
        y = x_ref[...] + acc_ref[...]
        if final_norm:
            y = _rms(y, nf_ref[...])
        o_ref[...] = y


def ffn(x, nw, wgu_stack, wd_stack, layer, nf, final_norm):
    m, d = x.shape
    tm = min(m, 1024)
    tf = D_FF // 2
    nj = D_FF // tf
    return pl.pallas_call(
        functools.partial(_ffn_kernel, final_norm),
        grid=(m // tm, nj),
        in_specs=[
            pl.BlockSpec((tm, d), lambda i, j: (i, 0)),
            pl.BlockSpec((1, d), lambda i, j: (0, 0)),
            pl.BlockSpec((None, d, tf), lambda i, j: (layer, 0, j)),
            pl.BlockSpec((None, d, tf), lambda i, j: (layer, 0, j + nj)),
            pl.BlockSpec((None, tf, d), lambda i, j: (layer, j, 0)),
            pl.BlockSpec((1, d), lambda i, j: (0, 0)),
        ],
        out_specs=pl.BlockSpec((tm, d), lambda i, j: (i, 0)),
        out_shape=jax.ShapeDtypeStruct((m, d), f32),
        scratch_shapes=[pltpu.VMEM((tm, d), bf16), pltpu.VMEM((tm, d), f32)],
        compiler_params=_params("parallel", "arbitrary"),
        name="ffn",
    )(x, nw.reshape(1, d), wgu_stack, wgu_stack, wd_stack, nf.reshape(1, d))


def _split3(x):
    hi = x.astype(bf16)
    r1 = x - hi.astype(f32)
    mid = r1.astype(bf16)
    lo = (r1 - mid.astype(f32)).astype(bf16)
    return hi, mid, lo


def _delta_kernel(C, NS, n_iter, has_init, n_alias, *refs):
    proj_ref = refs[0]
    pos = 1
    if has_init:
        conv0_ref, s0_ref = refs[1:3]
        pos = 3
    cw_ref, alog_ref, dtb_ref, nw_ref = refs[pos:pos + 4]
    pos += 4 + n_alias
    o_ref, convout_ref, sout_ref = refs[pos:pos + 3]
    full_ref, qn_ref, kn_ref = refs[pos + 3:]
    n = pl.program_id(1)
    H = DN_HEAD
    NH = DN_V_HEADS
    NQ = DN_QK_HEADS
    per = NH // NQ
    seqs = range(NS)
    pairs = [(s, h) for s in seqs for h in range(NH)]
    qkp = [(s, qh) for s in seqs for qh in range(NQ)]
    P = range(len(pairs))
    qi = [s * NQ + h // per for s, h in pairs]

    @pl.when(n == 0)
    def _():
        if has_init:
            full_ref[:, 5:8, :] = conv0_ref[...]
            sout_ref[...] = s0_ref[...]
        else:
            full_ref[:, 5:8, :] = jnp.zeros((NS, DN_CONV - 1, DN_CONV_DIM), f32)
            sout_ref[...] = jnp.zeros_like(sout_ref)

    for s in seqs:
        full_ref[s, 8:8 + C, :] = proj_ref[s, :, 0:DN_CONV_DIM]

    def conv_silu(s, cb):
        cols = slice(cb * H, (cb + 1) * H)
        w = cw_ref[:, cols]
        y = (full_ref[s, 5:5 + C, cols] * w[0:1] + full_ref[s, 6:6 + C, cols] * w[1:2]
             + full_ref[s, 7:7 + C, cols] * w[2:3] + full_ref[s, 8:8 + C, cols] * w[3:4])
        return y * _sigmoid(y)

    for s in seqs:
        for cb in range(2 * NQ):
            y = conv_silu(s, cb)
            inv = lax.rsqrt(jnp.sum(y * y, axis=-1, keepdims=True) + L2_EPS)
            if cb < NQ:
                qn_ref[s * NQ + cb] = y * (inv * (H ** -0.5))
            else:
                kn_ref[s * NQ + cb - NQ] = y * inv

    ri = lax.broadcasted_iota(jnp.int32, (C, C), 0)
    ci = lax.broadcasted_iota(jnp.int32, (C, C), 1)
    causal = ri >= ci
    strict = ri > ci
    eye = jnp.where(ri == ci, 1.0, 0.0).astype(f32)
    tri16 = jnp.where(causal, 1.0, 0.0).astype(bf16)
    beta, G, GT = [], [], []
    for s in seqs:
        ba = proj_ref[s, :, DN_CONV_DIM + DN_VAL_DIM:DN_IN_PAD]
        beta.append(_sigmoid(ba[:, 0:NH]))
        xa = ba[:, NH:2 * NH] + dtb_ref[...]
        softplus = jnp.maximum(xa, 0.0) + jnp.log(1.0 + jnp.exp(-jnp.abs(xa)))
        g = -jnp.exp(alog_ref[...]) * softplus
        g3 = jnp.concatenate(_split3(g), axis=1)
        G3 = _dot(tri16, g3)
        G.append(G3[:, 0:NH] + G3[:, NH:2 * NH] + G3[:, 2 * NH:3 * NH])
        GT3 = lax.dot_general(g3, tri16, (((0,), (1,)), ((), ())), preferred_element_type=f32)
        GT.append(GT3[0:NH] + GT3[NH:2 * NH] + GT3[2 * NH:3 * NH])

    ks = [kn_ref[i] for i in range(len(qkp))]
    qs = [qn_ref[i] for i in range(len(qkp))]
    kqs = [_dot_nt(jnp.concatenate([ks[i], qs[i]], axis=0).astype(bf16), ks[i].astype(bf16))
           for i in range(len(qkp))]
    kTs = [ks[i].T for i in range(len(qkp))]
    Gc = [G[s][:, h:h + 1] for s, h in pairs]
    Gr = [GT[s][h:h + 1, :] for s, h in pairs]
    bc = [beta[s][:, h:h + 1] for s, h in pairs]
    decay = [jnp.where(causal, jnp.exp(jnp.where(causal, Gc[p] - Gr[p], 0.0)), 0.0) for p in P]
    M = [jnp.where(strict, -(bc[p] * kqs[qi[p]][:C] * decay[p]), 0.0) for p in P]
    T = [eye + M[p] for p in P]
    Mp = [_dot(M[p].astype(bf16), M[p].astype(bf16)) for p in P]
    for it in range(n_iter):
        if it < n_iter - 1:
            tm = [_dot(jnp.concatenate([T[p], Mp[p]], axis=0).astype(bf16), Mp[p].astype(bf16)) for p in P]
            T = [T[p] + tm[p][:C] for p in P]
            Mp = [tm[p][C:] for p in P]
        else:
            tm = [_dot(T[p].astype(bf16), Mp[p].astype(bf16)) for p in P]
            T = [T[p] + tm[p] for p in P]
    vs = [conv_silu(s, 2 * NQ + h) for s, h in pairs]
    for s in seqs:
        tail = full_ref[s, C + 5:C + 8, :]
        full_ref[s, 5:8, :] = tail
        convout_ref[s] = tail

    eG = [jnp.exp(Gc[p]) for p in P]
    uw = [_dot(T[p].astype(bf16),
               jnp.concatenate([vs[p] * bc[p], ks[qi[p]] * (bc[p] * eG[p])], axis=1).astype(bf16))
          for p in P]
    S = [sout_ref[s, h] for s, h in pairs]
    ws = [_dot(jnp.concatenate([uw[p][:, H:], qs[qi[p]] * eG[p]], axis=0).astype(bf16), S[p].astype(bf16))
          for p in P]
    unew = [uw[p][:, :H] - ws[p][:C] for p in P]
    Gl = [Gc[p][C - 1:C, :] for p in P]
    r = [_dot(jnp.concatenate([kqs[qi[p]][C:] * decay[p], kTs[qi[p]] * jnp.exp(Gl[p] - Gr[p])],
                              axis=0).astype(bf16), unew[p].astype(bf16))
         for p in P]
    for p, (s, h) in enumerate(pairs):
        sout_ref[s, h] = S[p] * jnp.exp(Gl[p]) + r[p][C:]
        on = _rms(ws[p][C:] + r[p][:C], nw_ref[...])
        z = proj_ref[s, :, DN_CONV_DIM + h * H:DN_CONV_DIM + (h + 1) * H]
        o_ref[s, :, h * H:(h + 1) * H] = (on * (z * _sigmoid(z))).astype(bf16)


def delta_mixer(proj, init, prev_out, layer, n_layers, conv_w, a_log, dt_bias, norm_w, C, NS):
    B, L, _ = proj.shape
    n_iter = max(1, (C - 1).bit_length()) - 1
    H = DN_HEAD
    has_init = init is not None
    n_alias = 0 if prev_out is None else 2
    conv_spec = pl.BlockSpec((None, NS, DN_CONV - 1, DN_CONV_DIM), lambda b, n: (layer, b, 0, 0))
    state_spec = pl.BlockSpec((None, NS, DN_V_HEADS, H, H), lambda b, n: (layer, b, 0, 0, 0))
    small = lambda shape: pl.BlockSpec(shape, lambda b, n: (0,) * len(shape))
    in_specs = [pl.BlockSpec((NS, C, DN_IN_PAD), lambda b, n: (b, n, 0))]
    args = [proj]
    if has_init:
        in_specs += [conv_spec, state_spec]
        args += list(init)
    in_specs += [small((DN_CONV, DN_CONV_DIM)), small((1, DN_V_HEADS)), small((1, DN_V_HEADS)), small((1, H))]
    args += [conv_w, a_log.reshape(1, -1), dt_bias.reshape(1, -1), norm_w.reshape(1, -1)]
    aliases = {}
    if prev_out is not None:
        aliases = {len(args): 1, len(args) + 1: 2}
        in_specs += [pl.BlockSpec(memory_space=pl.ANY), pl.BlockSpec(memory_space=pl.ANY)]
        args += list(prev_out)
    return pl.pallas_call(
        functools.partial(_delta_kernel, C, NS, n_iter, has_init, n_alias),
        grid=(B // NS, L // C),
        in_specs=in_specs,
        out_specs=[
            pl.BlockSpec((NS, C, DN_VAL_DIM), lambda b, n: (b, n, 0)),
            conv_spec,
            state_spec,
        ],
        out_shape=[
            jax.ShapeDtypeStruct((B, L, DN_VAL_DIM), bf16),
            jax.ShapeDtypeStruct((n_layers, B, DN_CONV - 1, DN_CONV_DIM), f32),
            jax.ShapeDtypeStruct((n_layers, B, DN_V_HEADS, H, H), f32),
        ],
        scratch_shapes=[
            pltpu.VMEM((NS, C + 8, DN_CONV_DIM), f32),
            pltpu.VMEM((NS * DN_QK_HEADS, C, H), f32),
            pltpu.VMEM((NS * DN_QK_HEADS, C, H), f32),
        ],
        input_output_aliases=aliases,
        compiler_params=_params("parallel", "arbitrary"),
        name="delta_mixer",
    )(*args)


def _swa_prompt_kernel(sinks_ref, q_ref, kc_ref, kp_ref, vc_ref, vp_ref, o_ref):
    n = pl.program_id(1)
    W = WINDOW
    hd = SWA_HEAD_DIM
    ri = lax.broadcasted_iota(jnp.int32, (W, 2 * W), 0)
    ci = lax.broadcasted_iota(jnp.int32, (W, 2 * W), 1)
    rel = W + ri - ci
    first_col = jnp.where(n > 0, 0, W)
    valid = (rel >= 0) & (rel <= WINDOW) & (ci >= first_col)
    relf = rel.astype(f32)
    k16, v16 = [], []
    for kv in range(SWA_KV_HEADS):
        cs = slice(kv * hd, (kv + 1) * hd)
        k16.append(jnp.concatenate([kp_ref[0, :, cs], kc_ref[0, :, cs]], axis=0).astype(bf16))
        v16.append(jnp.concatenate([vp_ref[0, :, cs], vc_ref[0, :, cs]], axis=0).astype(bf16))
    heads = range(SWA_HEADS)
    s = [_dot_nt(q_ref[0, :, h * hd:(h + 1) * hd].astype(bf16), k16[h // SWA_GROUPS]) for h in heads]
    p = []
    for h in heads:
        sh = jnp.where(valid, s[h] * SWA_SCALE - ALIBI_SLOPES[h] * relf, -jnp.inf)
        sk = sinks_ref[h]
        m = jnp.maximum(jnp.max(sh, axis=-1, keepdims=True), sk)
        e = jnp.exp(sh - m)
        denom = jnp.sum(e, axis=-1, keepdims=True) + jnp.exp(sk - m)
        p.append((e / denom).astype(bf16))
    o = [_dot(p[h], v16[h // SWA_GROUPS]) for h in heads]
    for h in heads:
        o_ref[0, :, h * hd:(h + 1) * hd] = o[h].astype(bf16)


def swa_prompt_attention(qkv, sinks):
    B, L, _ = qkv.shape
    W = WINDOW
    kb = SWA_Q_DIM // SWA_KV_DIM
    prev = lambda n: jnp.maximum(n - 1, 0)
    return pl.pallas_call(
        _swa_prompt_kernel,
        grid=(B, L // W),
        in_specs=[
            pl.BlockSpec(memory_space=pltpu.SMEM),
            pl.BlockSpec((1, W, SWA_Q_DIM), lambda b, n: (b, n, 0)),
            pl.BlockSpec((1, W, SWA_KV_DIM), lambda b, n: (b, n, kb)),
            pl.BlockSpec((1, W, SWA_KV_DIM), lambda b, n: (b, prev(n), kb)),
            pl.BlockSpec((1, W, SWA_KV_DIM), lambda b, n: (b, n, kb + 1)),
            pl.BlockSpec((1, W, SWA_KV_DIM), lambda b, n: (b, prev(n), kb + 1)),
        ],
        out_specs=pl.BlockSpec((1, W, SWA_Q_DIM), lambda b, n: (b, n, 0)),
        out_shape=jax.ShapeDtypeStruct((B, L, SWA_Q_DIM), bf16),
        compiler_params=_params("parallel", "arbitrary"),
        name="swa_prompt",
    )(sinks, qkv, qkv, qkv, qkv, qkv)


def _swa_sample_kernel(NS, L, R, sinks_ref, qkv_ref, ck_ref, cv_ref, o_ref):
    hd = SWA_HEAD_DIM
    G = SWA_GROUPS
    rows = G * L
    t_c = lax.broadcasted_iota(jnp.int32, (rows, R), 0) % L
    j_c = lax.broadcasted_iota(jnp.int32, (rows, R), 1)
    rel_c = R + t_c - j_c
    valid_c = (rel_c >= 0) & (rel_c <= WINDOW)
    t_n = lax.broadcasted_iota(jnp.int32, (rows, L), 0) % L
    j_n = lax.broadcasted_iota(jnp.int32, (rows, L), 1)
    rel_n = t_n - j_n
    valid_n = (rel_n >= 0) & (rel_n <= WINDOW)
    g_row = lax.broadcasted_iota(jnp.int32, (rows, 1), 0) // L
    bias_c, bias_n, sks = [], [], []
    for kv in range(SWA_KV_HEADS):
        slope = jnp.zeros((rows, 1), f32)
        sk = jnp.zeros((rows, 1), f32)
        for gi in range(G):
            h = kv * G + gi
            slope = jnp.where(g_row == gi, ALIBI_SLOPES[h], slope)
            sk = jnp.where(g_row == gi, sinks_ref[h], sk)
        bias_c.append(slope * rel_c.astype(f32))
        bias_n.append(slope * rel_n.astype(f32))
        sks.append(sk)
    pairs = [(b, kv) for b in range(NS) for kv in range(SWA_KV_HEADS)]
    kcol = lambda kv: slice(SWA_Q_DIM + kv * hd, SWA_Q_DIM + (kv + 1) * hd)
    vcol = lambda kv: slice(SWA_Q_DIM + SWA_KV_DIM + kv * hd, SWA_Q_DIM + SWA_KV_DIM + (kv + 1) * hd)
    q16 = [jnp.concatenate([qkv_ref[b, :, (kv * G + gi) * hd:(kv * G + gi + 1) * hd] for gi in range(G)],
                           axis=0).astype(bf16) for b, kv in pairs]
    s_c = [_dot_nt(q16[i], ck_ref[b, :, kv * hd:(kv + 1) * hd].astype(bf16)) for i, (b, kv) in enumerate(pairs)]
    s_n = [_dot_nt(q16[i], qkv_ref[b, :, kcol(kv)].astype(bf16)) for i, (b, kv) in enumerate(pairs)]
    p_c, p_n = [], []
    for i, (b, kv) in enumerate(pairs):
        sc = jnp.where(valid_c, s_c[i] * SWA_SCALE - bias_c[kv], -jnp.inf)
        sn = jnp.where(valid_n, s_n[i] * SWA_SCALE - bias_n[kv], -jnp.inf)
        m = jnp.maximum(jnp.maximum(jnp.max(sc, axis=-1, keepdims=True),
                                    jnp.max(sn, axis=-1, keepdims=True)), sks[kv])
        ec = jnp.exp(sc - m)
        en = jnp.exp(sn - m)
        denom = jnp.sum(ec, axis=-1, keepdims=True) + jnp.sum(en, axis=-1, keepdims=True) + jnp.exp(sks[kv] - m)
        p_c.append((ec / denom).astype(bf16))
        p_n.append((en / denom).astype(bf16))
    o = [_dot(p_c[i], cv_ref[b, :, kv * hd:(kv + 1) * hd].astype(bf16))
         + _dot(p_n[i], qkv_ref[b, :, vcol(kv)].astype(bf16)) for i, (b, kv) in enumerate(pairs)]
    for i, (b, kv) in enumerate(pairs):
        for gi in range(G):
            h = kv * G + gi
            o_ref[b, :, h * hd:(h + 1) * hd] = o[i][gi * L:(gi + 1) * L].astype(bf16)


def swa_sample_attention(qkv, cache_k, cache_v, sinks):
    B, L, _ = qkv.shape
    R = cache_k.shape[1]
    NS = min(B, SWA_SAMPLE_SEQS)
    return pl.pallas_call(
        functools.partial(_swa_sample_kernel, NS, L, R),
        grid=(B // NS,),
        in_specs=[
            pl.BlockSpec(memory_space=pltpu.SMEM),
            pl.BlockSpec((NS, L, SWA_QKV_DIM), lambda b: (b, 0, 0)),
            pl.BlockSpec((NS, R, SWA_KV_DIM), lambda b: (b, 0, 0)),
            pl.BlockSpec((NS, R, SWA_KV_DIM), lambda b: (b, 0, 0)),
        ],
        out_specs=pl.BlockSpec((NS, L, SWA_Q_DIM), lambda b: (b, 0, 0)),
        out_shape=jax.ShapeDtypeStruct((B, L, SWA_Q_DIM), bf16),
        compiler_params=_params("parallel"),
        name="swa_sample",
    )(sinks, qkv, cache_k, cache_v)


def kernel(x_prompt, x_sample, state_conv, state_delta, cache_k, cache_v, norm_mix, norm_ffn, norm_final,
           dn_w_in, dn_conv_w, dn_a_log, dn_dt_bias, dn_norm_w, dn_w_out, swa_w_qkv, swa_b_qkv, swa_sinks,
           swa_w_o, swa_b_o, ffn_w_gu, ffn_w_down):
    Bp, Lp, D = x_prompt.shape
    Bs, Ls, _ = x_sample.shape
    depth = norm_mix.shape[0]
    n_delta = dn_w_in.shape[0]
    yp = x_prompt.reshape(Bp * Lp, D)
    ys = x_sample.reshape(Bs * Ls, D)
    R = cache_k.shape[2]

    w_in = jnp.pad(dn_w_in.astype(bf16), ((0, 0), (0, 0), (0, DN_IN_PAD - DN_IN_DIM)))
    w_out = dn_w_out.astype(bf16)
    w_qkv = swa_w_qkv.astype(bf16)
    w_o = swa_w_o.astype(bf16)
    w_gu = ffn_w_gu.astype(bf16)
    w_down = ffn_w_down.astype(bf16)
    zero_b = jnp.zeros((D,), f32)
    zero_in = jnp.zeros((DN_IN_PAD,), f32)

    kp_list, vp_list, ks_list, vs_list = [], [], [], []
    prompt_states = None
    sample_states = None
    for i in range(depth):
        j = i // 2
        if i % 2 == 0:
            wts = (dn_conv_w[j], dn_a_log[j], dn_dt_bias[j], dn_norm_w[j])
            proj = rms_matmul(yp, norm_mix[i], w_in, j, zero_in, 896).reshape(Bp, Lp, DN_IN_PAD)
            o, cb, sn = delta_mixer(proj, None, prompt_states, j, n_delta, *wts, C=min(DN_CHUNK, Lp), NS=1)
            prompt_states = (cb, sn)
            yp = matmul_resid(o.reshape(Bp * Lp, DN_VAL_DIM), w_out, j, zero_b, yp)
            proj = rms_matmul(ys, norm_mix[i], w_in, j, zero_in, 896).reshape(Bs, Ls, DN_IN_PAD)
            o, cb, sn = delta_mixer(proj, (state_conv, state_delta), sample_states, j, n_delta, *wts, C=Ls,
                                    NS=min(Bs, DN_SAMPLE_SEQS))
            sample_states = (cb, sn)
            ys = matmul_resid(o.reshape(Bs * Ls, DN_VAL_DIM), w_out, j, zero_b, ys)
        else:
            qkv = rms_matmul(yp, norm_mix[i], w_qkv, j, swa_b_qkv[j], 768).reshape(Bp, Lp, SWA_QKV_DIM)
            o = swa_prompt_attention(qkv, swa_sinks[j])
            yp = matmul_resid(o.reshape(Bp * Lp, SWA_Q_DIM), w_o, j, swa_b_o[j], yp)
            Rp = min(WINDOW, Lp)
            kp_list.append(qkv[:, Lp - Rp:, SWA_Q_DIM:SWA_Q_DIM + SWA_KV_DIM].reshape(Bp, Rp, SWA_KV_HEADS, SWA_HEAD_DIM))
            vp_list.append(qkv[:, Lp - Rp:, SWA_Q_DIM + SWA_KV_DIM:].reshape(Bp, Rp, SWA_KV_HEADS, SWA_HEAD_DIM))
            qkv = rms_matmul(ys, norm_mix[i], w_qkv, j, swa_b_qkv[j], 768).reshape(Bs, Ls, SWA_QKV_DIM)
            o = swa_sample_attention(qkv, cache_k[j].reshape(Bs, R, SWA_KV_DIM), cache_v[j].reshape(Bs, R, SWA_KV_DIM),
                                     swa_sinks[j])
            ys = matmul_resid(o.reshape(Bs * Ls, SWA_Q_DIM), w_o, j, swa_b_o[j], ys)
            ks_list.append(qkv[:, :, SWA_Q_DIM:SWA_Q_DIM + SWA_KV_DIM].reshape(Bs, Ls, SWA_KV_HEADS, SWA_HEAD_DIM))
            vs_list.append(qkv[:, :, SWA_Q_DIM + SWA_KV_DIM:].reshape(Bs, Ls, SWA_KV_HEADS, SWA_HEAD_DIM))
        last = i == depth - 1
        yp = ffn(yp, norm_ffn[i], w_gu, w_down, i, norm_final, last)
        ys = ffn(ys, norm_ffn[i], w_gu, w_down, i, norm_final, last)

    return (yp.reshape(Bp, Lp, D), ys.reshape(Bs, Ls, D),
            prompt_states[0], prompt_states[1], jnp.stack(kp_list), jnp.stack(vp_list),
            sample_states[0], sample_states[1], jnp.stack(ks_list), jnp.stack(vs_list))
```

```python
import functools

import jax
import jax.numpy as jnp
from jax import lax
from jax.experimental import pallas as pl
from jax.experimental.pallas import tpu as pltpu

f32 = jnp.float32
bf16 = jnp.bfloat16

D_MODEL = 1024
DN_QK_HEADS = 8
DN_V_HEADS = 16
DN_HEAD = 128
DN_KEY_DIM = DN_QK_HEADS * DN_HEAD
DN_VAL_DIM = DN_V_HEADS * DN_HEAD
DN_CONV_DIM = 2 * DN_KEY_DIM + DN_VAL_DIM
DN_CONV = 4
DN_IN_DIM = DN_CONV_DIM + DN_VAL_DIM + 2 * DN_V_HEADS
DN_IN_PAD = 6272
DN_ZBA_PAD = DN_IN_PAD - DN_CONV_DIM
SWA_HEADS = 16
SWA_KV_HEADS = 4
SWA_GROUPS = SWA_HEADS // SWA_KV_HEADS
SWA_HEAD_DIM = 64
SWA_Q_DIM = SWA_HEADS * SWA_HEAD_DIM
SWA_KV_DIM = SWA_KV_HEADS * SWA_HEAD_DIM
SWA_QKV_DIM = SWA_Q_DIM + 2 * SWA_KV_DIM
SWA_SCALE = SWA_HEAD_DIM ** -0.5
WINDOW = 128
D_FF = 2816
RMS_EPS = 1e-6
L2_EPS = 1e-6
ALIBI_SLOPES = tuple(float(2.0 ** (-8.0 * (i + 1) / SWA_HEADS)) for i in range(SWA_HEADS))

VMEM_LIMIT_BYTES = 56 * 1024 * 1024
DN_CHUNK = 64
SWA_SAMPLE_SEQS = 8
DN_SAMPLE_SEQS = 4
SWA_HEAD_GROUP = 16


def _params(*sem):
    return pltpu.CompilerParams(dimension_semantics=sem, vmem_limit_bytes=VMEM_LIMIT_BYTES)


def _sigmoid(x):
    return 1.0 / (1.0 + jnp.exp(-x))


def _silu(x):
    h = 0.5 * x
    return h + h * jnp.tanh(h)


def _rms(x, w):
    return x * lax.rsqrt(jnp.mean(x * x, axis=-1, keepdims=True) + RMS_EPS) * w


def _dot(a, b):
    return jnp.dot(a, b, preferred_element_type=f32)


def _dot_nt(a, b):
    return lax.dot_general(a, b, (((1,), (1,)), ((), ())), preferred_element_type=f32)


def _dot_tn(a, b):
    return lax.dot_general(a, b, (((0,), (0,)), ((), ())), preferred_element_type=f32)


def _rms_matmul_kernel(x_ref, nw_ref, w_ref, b_ref, o_ref, xn_ref):
    @pl.when(pl.program_id(1) == 0)
    def _():
        xn_ref[...] = _rms(x_ref[...], nw_ref[...]).astype(bf16)

    o_ref[...] = _dot(xn_ref[...], w_ref[...]) + b_ref[...]


def rms_matmul(x, nw, w_stack, layer, b, tn):
    m, d = x.shape
    n = w_stack.shape[2]
    tm = min(m, 1024)
    return pl.pallas_call(
        _rms_matmul_kernel,
        grid=(m // tm, n // tn),
        in_specs=[
            pl.BlockSpec((tm, d), lambda i, j: (i, 0)),
            pl.BlockSpec((1, d), lambda i, j: (0, 0)),
            pl.BlockSpec((None, d, tn), lambda i, j: (layer, 0, j)),
            pl.BlockSpec((1, tn), lambda i, j: (0, j)),
        ],
        out_specs=pl.BlockSpec((tm, tn), lambda i, j: (i, j)),
        out_shape=jax.ShapeDtypeStruct((m, n), f32),
        scratch_shapes=[pltpu.VMEM((tm, d), bf16)],
        compiler_params=_params("parallel", "arbitrary"),
        name="rms_matmul",
    )(x, nw.reshape(1, d), w_stack, b.reshape(1, n))


def _proj_conv_kernel(tiles_per_seq, RS, x_ref, nw_ref, w_ref, cw_ref, o_ref, cs_ref, xn_ref, ext_ref, carry_ref):
    i = pl.program_id(0)
    j = pl.program_id(1)
    H = DN_HEAD
    tm, tn = o_ref.shape

    @pl.when(j == 0)
    def _():
        xn_ref[...] = _rms(x_ref[...], nw_ref[...]).astype(bf16)

    ext_ref[0:8, :] = jnp.where(i % tiles_per_seq == 0, 0.0, carry_ref[j])
    cw = cw_ref[...]
    for r in range(tm // RS):
        rows = slice(r * RS, (r + 1) * RS)
        a = _dot(xn_ref[rows, :], w_ref[...])
        ext_ref[8:8 + RS, :] = a
        y = _silu(ext_ref[5:5 + RS, :] * cw[0:1] + ext_ref[6:6 + RS, :] * cw[1:2]
                  + ext_ref[7:7 + RS, :] * cw[2:3] + a * cw[3:4])
        for hh in range(tn // H):
            yh = y[:, hh * H:(hh + 1) * H]
            inv = lax.rsqrt(jnp.sum(yh * yh, axis=-1, keepdims=True) + L2_EPS)
            fac = jnp.where(j == 0, inv * (H ** -0.5), jnp.where(j == 1, inv, 1.0))
            o_ref[rows, hh * H:(hh + 1) * H] = yh * fac
        ext_ref[0:8, :] = ext_ref[RS:RS + 8, :]
    carry_ref[j] = ext_ref[0:8, :]
    cs_ref[0] = ext_ref[5:8, :]


def proj_conv(x, nw, w_stack, layer, conv_w, seq_len):
    m, d = x.shape
    tm, tn, RS = 1024, DN_KEY_DIM, 512
    assert seq_len % tm == 0 and m % seq_len == 0
    tiles_per_seq = seq_len // tm
    nj = DN_CONV_DIM // tn
    qkv, tails = pl.pallas_call(
        functools.partial(_proj_conv_kernel, tiles_per_seq, RS),
        grid=(m // tm, nj),
        in_specs=[
            pl.BlockSpec((tm, d), lambda i, j: (i, 0)),
            pl.BlockSpec((1, d), lambda i, j: (0, 0)),
            pl.BlockSpec((None, d, tn), lambda i, j: (layer, 0, j)),
            pl.BlockSpec((DN_CONV, tn), lambda i, j: (0, j)),
        ],
        out_specs=[
            pl.BlockSpec((tm, tn), lambda i, j: (i, j)),
            pl.BlockSpec((1, DN_CONV - 1, tn), lambda i, j: (i, 0, j)),
        ],
        out_shape=[
            jax.ShapeDtypeStruct((m, DN_CONV_DIM), f32),
            jax.ShapeDtypeStruct((m // tm, DN_CONV - 1, DN_CONV_DIM), f32),
        ],
        scratch_shapes=[
            pltpu.VMEM((tm, d), bf16),
            pltpu.VMEM((RS + 8, tn), f32),
            pltpu.VMEM((nj, 8, tn), f32),
        ],
        compiler_params=_params("arbitrary", "arbitrary"),
        name="proj_conv",
    )(x, nw.reshape(1, d), w_stack, conv_w)
    return qkv, tails.reshape(m // seq_len, tiles_per_seq, DN_CONV - 1, DN_CONV_DIM)[:, -1]


def _matmul_resid_kernel(a_ref, w_ref, b_ref, r_ref, o_ref):
    o_ref[...] = r_ref[...] + _dot(a_ref[...], w_ref[...]) + b_ref[...]


def matmul_resid(a, w_stack, layer, b, resid):
    m, k = a.shape
    n = w_stack.shape[2]
    tm = min(m, 1024)
    return pl.pallas_call(
        _matmul_resid_kernel,
        grid=(m // tm,),
        in_specs=[
            pl.BlockSpec((tm, k), lambda i: (i, 0)),
            pl.BlockSpec((None, k, n), lambda i: (layer, 0, 0)),
            pl.BlockSpec((1, n), lambda i: (0, 0)),
            pl.BlockSpec((tm, n), lambda i: (i, 0)),
        ],
        out_specs=pl.BlockSpec((tm, n), lambda i: (i, 0)),
        out_shape=jax.ShapeDtypeStruct((m, n), f32),
        compiler_params=_params("parallel"),
        name="matmul_resid",
    )(a, w_stack, b.reshape(1, n), resid)


def _ffn_kernel(final_norm, x_ref, nw_ref, wg_ref, wu_ref, wd_ref, nf_ref, o_ref, xn_ref, acc_ref):
    j = pl.program_id(1)

    @pl.when(j == 0)
    def _():
        xn_ref[...] = _rms(x_ref[...], nw_ref[...]).astype(bf16)
        acc_ref[...] = jnp.zeros_like(acc_ref)

    xn = xn_ref[...]
    tf = wg_ref.shape[1]
    half = (tf // 256) * 128
    for c0, c1 in ((0, half), (half, tf)):
        g = _dot(xn, wg_ref[:, c0:c1])
        u = _dot(xn, wu_ref[:, c0:c1])
        h = (_silu(g) * u).astype(bf16)
        acc_ref[...] += _dot(h, wd_ref[c0:c1, :])

    @pl.when(j == pl.num_programs(1) - 1)
    def _():
        y = x_ref[...] + acc_ref[...]
        if final_norm:
            y = _rms(y, nf_ref[...])
        o_ref[...] = y


def ffn(x, nw, wgu_stack, wd_stack, layer, nf, final_norm):
    m, d = x.shape
    tm = min(m, 1024)
    tf = D_FF // 2
    nj = D_FF // tf
    return pl.pallas_call(
        functools.partial(_ffn_kernel, final_norm),
        grid=(m // tm, nj),
        in_specs=[
            pl.BlockSpec((tm, d), lambda i, j: (i, 0)),
            pl.BlockSpec((1, d), lambda i, j: (0, 0)),
            pl.BlockSpec((None, d, tf), lambda i, j: (layer, 0, j)),
            pl.BlockSpec((None, d, tf), lambda i, j: (layer, 0, j + nj)),
            pl.BlockSpec((None, tf, d), lambda i, j: (layer, j, 0)),
            pl.BlockSpec((1, d), lambda i, j: (0, 0)),
        ],
        out_specs=pl.BlockSpec((tm, d), lambda i, j: (i, 0)),
        out_shape=jax.ShapeDtypeStruct((m, d), f32),
        scratch_shapes=[pltpu.VMEM((tm, d), bf16), pltpu.VMEM((tm, d), f32)],
        compiler_params=_params("parallel", "arbitrary"),
        name="ffn",
    )(x, nw.reshape(1, d), wgu_stack, wgu_stack, wd_stack, nf.reshape(1, d))


def _split3(x):
    hi = x.astype(bf16)
    r1 = x - hi.astype(f32)
    mid = r1.astype(bf16)
    lo = (r1 - mid.astype(f32)).astype(bf16)
    return hi, mid, lo


def _delta_kernel(C, NS, n_iter, has_init, n_alias, pre_conv, *refs):
    if pre_conv:
        qkv_ref, zba_ref = refs[0:2]
        pos = 2
    else:
        proj_ref = refs[0]
        pos = 1
    if has_init:
        conv0_ref, s0_ref = refs[pos:pos + 2]
        pos += 2
    cw_ref, alog_ref, dtb_ref, nw_ref = refs[pos:pos + 4]
    pos += 4 + n_alias
    if pre_conv:
        o_ref, sout_ref = refs[pos:pos + 2]
    else:
        o_ref, convout_ref, sout_ref = refs[pos:pos + 3]
        full_ref, qn_ref, kn_ref = refs[pos + 3:]
    n = pl.program_id(1)
    H = DN_HEAD
    NH = DN_V_HEADS
    NQ = DN_QK_HEADS
    per = NH // NQ
    seqs = range(NS)
    pairs = [(s, h) for s in seqs for h in range(NH)]
    qkp = [(s, qh) for s in seqs for qh in range(NQ)]
    P = range(len(pairs))
    qi = [s * NQ + h // per for s, h in pairs]

    @pl.when(n == 0)
    def _():
        if has_init:
            full_ref[:, 5:8, :] = conv0_ref[...]
            sout_ref[...] = s0_ref[...]
        else:
            if not pre_conv:
                full_ref[:, 5:8, :] = jnp.zeros((NS, DN_CONV - 1, DN_CONV_DIM), f32)
            sout_ref[...] = jnp.zeros_like(sout_ref)

    if pre_conv:
        get_ba = lambda s: zba_ref[s, :, DN_VAL_DIM:DN_VAL_DIM + H]
        get_z = lambda s, h: zba_ref[s, :, h * H:(h + 1) * H]
    else:
        get_ba = lambda s: proj_ref[s, :, DN_CONV_DIM + DN_VAL_DIM:DN_IN_PAD]
        get_z = lambda s, h: proj_ref[s, :, DN_CONV_DIM + h * H:DN_CONV_DIM + (h + 1) * H]
        for s in seqs:
            full_ref[s, 8:8 + C, :] = proj_ref[s, :, 0:DN_CONV_DIM]

    def conv_silu(s, cb):
        cols = slice(cb * H, (cb + 1) * H)
        w = cw_ref[:, cols]
        y = (full_ref[s, 5:5 + C, cols] * w[0:1] + full_ref[s, 6:6 + C, cols] * w[1:2]
             + full_ref[s, 7:7 + C, cols] * w[2:3] + full_ref[s, 8:8 + C, cols] * w[3:4])
        return _silu(y)

    if not pre_conv:
        for s in seqs:
            for cb in range(2 * NQ):
                y = conv_silu(s, cb)
                inv = lax.rsqrt(jnp.sum(y * y, axis=-1, keepdims=True) + L2_EPS)
                if cb < NQ:
                    qn_ref[s * NQ + cb] = y * (inv * (H ** -0.5))
                else:
                    kn_ref[s * NQ + cb - NQ] = y * inv

    ri = lax.broadcasted_iota(jnp.int32, (C, C), 0)
    ci = lax.broadcasted_iota(jnp.int32, (C, C), 1)
    causal = ri >= ci
    strict = ri > ci
    eye = jnp.where(ri == ci, 1.0, 0.0).astype(f32)
    tri16 = jnp.where(causal, 1.0, 0.0).astype(bf16)
    beta, G, GT = [], [], []
    for s in seqs:
        ba = get_ba(s)
        beta.append(_sigmoid(ba[:, 0:NH]))
        xa = ba[:, NH:2 * NH] + dtb_ref[...]
        softplus = jnp.maximum(xa, 0.0) + jnp.log(1.0 + jnp.exp(-jnp.abs(xa)))
        g = -jnp.exp(alog_ref[...]) * softplus
        g3 = jnp.concatenate(_split3(g), axis=1)
        G3 = _dot(tri16, g3)
        G.append(G3[:, 0:NH] + G3[:, NH:2 * NH] + G3[:, 2 * NH:3 * NH])
        GT3 = lax.dot_general(g3, tri16, (((0,), (1,)), ((), ())), preferred_element_type=f32)
        GT.append(GT3[0:NH] + GT3[NH:2 * NH] + GT3[2 * NH:3 * NH])

    if pre_conv:
        qs = [qkv_ref[s, :, qh * H:(qh + 1) * H] for s, qh in qkp]
        ks = [qkv_ref[s, :, DN_KEY_DIM + qh * H:DN_KEY_DIM + (qh + 1) * H] for s, qh in qkp]
    else:
        ks = [kn_ref[i] for i in range(len(qkp))]
        qs = [qn_ref[i] for i in range(len(qkp))]
    kqs = [_dot_nt(jnp.concatenate([ks[i], qs[i]], axis=0).astype(bf16), ks[i].astype(bf16))
           for i in range(len(qkp))]
    kTs = [ks[i].T for i in range(len(qkp))]
    Gc = [G[s][:, h:h + 1] for s, h in pairs]
    Gr = [GT[s][h:h + 1, :] for s, h in pairs]
    bc = [beta[s][:, h:h + 1] for s, h in pairs]
    decay = [jnp.where(causal, jnp.exp(jnp.where(causal, Gc[p] - Gr[p], 0.0)), 0.0) for p in P]
    M = [jnp.where(strict, -(bc[p] * kqs[qi[p]][:C] * decay[p]), 0.0) for p in P]
    T = [eye + M[p] for p in P]
    Mp = [_dot(M[p].astype(bf16), M[p].astype(bf16)) for p in P]
    for it in range(n_iter):
        if it < n_iter - 1:
            tm = [_dot(jnp.concatenate([T[p], Mp[p]], axis=0).astype(bf16), Mp[p].astype(bf16)) for p in P]
            T = [T[p] + tm[p][:C] for p in P]
            Mp = [tm[p][C:] for p in P]
        else:
            tm = [_dot(T[p].astype(bf16), Mp[p].astype(bf16)) for p in P]
            T = [T[p] + tm[p] for p in P]
    if pre_conv:
        vs = [qkv_ref[s, :, 2 * DN_KEY_DIM + h * H:2 * DN_KEY_DIM + (h + 1) * H] for s, h in pairs]
    else:
        vs = [conv_silu(s, 2 * NQ + h) for s, h in pairs]
        for s in seqs:
            tail = full_ref[s, C + 5:C + 8, :]
            full_ref[s, 5:8, :] = tail
            convout_ref[s] = tail

    eG = [jnp.exp(Gc[p]) for p in P]
    uw = [_dot(T[p].astype(bf16),
               jnp.concatenate([vs[p] * bc[p], ks[qi[p]] * (bc[p] * eG[p])], axis=1).astype(bf16))
          for p in P]
    S = [sout_ref[s, h] for s, h in pairs]
    ws = [_dot(jnp.concatenate([uw[p][:, H:], qs[qi[p]] * eG[p]], axis=0).astype(bf16), S[p].astype(bf16))
          for p in P]
    unew = [uw[p][:, :H] - ws[p][:C] for p in P]
    Gl = [Gc[p][C - 1:C, :] for p in P]
    r = [_dot(jnp.concatenate([kqs[qi[p]][C:] * decay[p], kTs[qi[p]] * jnp.exp(Gl[p] - Gr[p])],
                              axis=0).astype(bf16), unew[p].astype(bf16))
         for p in P]
    for p, (s, h) in enumerate(pairs):
        sout_ref[s, h] = S[p] * jnp.exp(Gl[p]) + r[p][C:]
        on = _rms(ws[p][C:] + r[p][:C], nw_ref[...])
        o_ref[s, :, h * H:(h + 1) * H] = (on * _silu(get_z(s, h))).astype(bf16)


def delta_mixer(proj, init, prev_out, layer, n_layers, conv_w, a_log, dt_bias, norm_w, C, NS):
    pre_conv = isinstance(proj, tuple)
    assert not (pre_conv and init is not None)
    B, L, _ = proj[0].shape if pre_conv else proj.shape
    n_iter = max(1, (C - 1).bit_length()) - 1
    H = DN_HEAD
    has_init = init is not None
    n_alias = 0 if prev_out is None else len(prev_out)
    conv_spec = pl.BlockSpec((None, NS, DN_CONV - 1, DN_CONV_DIM), lambda b, n: (layer, b, 0, 0))
    state_spec = pl.BlockSpec((None, NS, DN_V_HEADS, H, H), lambda b, n: (layer, b, 0, 0, 0))
    small = lambda shape: pl.BlockSpec(shape, lambda b, n: (0,) * len(shape))
    if pre_conv:
        in_specs = [pl.BlockSpec((NS, C, DN_CONV_DIM), lambda b, n: (b, n, 0)),
                    pl.BlockSpec((NS, C, DN_ZBA_PAD), lambda b, n: (b, n, 0))]
        args = list(proj)
    else:
        in_specs = [pl.BlockSpec((NS, C, DN_IN_PAD), lambda b, n: (b, n, 0))]
        args = [proj]
    if has_init:
        in_specs += [conv_spec, state_spec]
        args += list(init)
    in_specs += [small((DN_CONV, DN_CONV_DIM)), small((1, DN_V_HEADS)), small((1, DN_V_HEADS)), small((1, H))]
    args += [conv_w, a_log.reshape(1, -1), dt_bias.reshape(1, -1), norm_w.reshape(1, -1)]
    out_specs = [pl.BlockSpec((NS, C, DN_VAL_DIM), lambda b, n: (b, n, 0)), conv_spec, state_spec]
    out_shape = [
        jax.ShapeDtypeStruct((B, L, DN_VAL_DIM), bf16),
        jax.ShapeDtypeStruct((n_layers, B, DN_CONV - 1, DN_CONV_DIM), f32),
        jax.ShapeDtypeStruct((n_layers, B, DN_V_HEADS, H, H), f32),
    ]
    scratch_shapes = [
        pltpu.VMEM((NS, C + 8, DN_CONV_DIM), f32),
        pltpu.VMEM((NS * DN_QK_HEADS, C, H), f32),
        pltpu.VMEM((NS * DN_QK_HEADS, C, H), f32),
    ]
    if pre_conv:
        del out_specs[1], out_shape[1]
        scratch_shapes = []
    aliases = {}
    if prev_out is not None:
        first_out = len(out_specs) - len(prev_out)
        aliases = {len(args) + k: first_out + k for k in range(len(prev_out))}
        in_specs += [pl.BlockSpec(memory_space=pl.ANY)] * len(prev_out)
        args += list(prev_out)
    return pl.pallas_call(
        functools.partial(_delta_kernel, C, NS, n_iter, has_init, n_alias, pre_conv),
        grid=(B // NS, L // C),
        in_specs=in_specs,
        out_specs=out_specs,
        out_shape=out_shape,
        scratch_shapes=scratch_shapes,
        input_output_aliases=aliases,
        compiler_params=_params("parallel", "arbitrary"),
        name="delta_mixer",
    )(*args)


def _swa_bias_table():
    W = WINDOW
    ri = lax.broadcasted_iota(jnp.int32, (W, 2 * W), 0)
    ci = lax.broadcasted_iota(jnp.int32, (W, 2 * W), 1)
    rel = W + ri - ci
    band = (rel >= 0) & (rel <= WINDOW)
    slopes = jnp.asarray(ALIBI_SLOPES, f32).reshape(SWA_HEADS, 1, 1)
    alibi = -slopes * rel.astype(f32)[None]
    later = jnp.where(band[None], alibi, -jnp.inf)
    first = jnp.where((band & (ci >= W))[None], alibi, -jnp.inf)
    return jnp.stack([first, later])


def _swa_prompt_kernel(NB, sinks_ref, bias_ref, q_ref, kc_ref, kp_ref, vc_ref, vp_ref, o_ref):
    hd = SWA_HEAD_DIM
    k16, v16 = [], []
    for b in range(NB):
        for kv in range(SWA_KV_HEADS):
            cs = slice(kv * hd, (kv + 1) * hd)
            k16.append(jnp.concatenate([kp_ref[b, :, cs], kc_ref[b, :, cs]], axis=0).astype(bf16))
            v16.append(jnp.concatenate([vp_ref[b, :, cs], vc_ref[b, :, cs]], axis=0).astype(bf16))
    all_pairs = [(b, h) for b in range(NB) for h in range(SWA_HEADS)]
    for g0 in range(0, len(all_pairs), SWA_HEAD_GROUP):
        pairs = all_pairs[g0:g0 + SWA_HEAD_GROUP]
        kvi = [b * SWA_KV_HEADS + h // SWA_GROUPS for b, h in pairs]
        s = [_dot_nt((q_ref[b, :, h * hd:(h + 1) * hd] * SWA_SCALE).astype(bf16), k16[kvi[i]])
             for i, (b, h) in enumerate(pairs)]
        p = []
        for i, (b, h) in enumerate(pairs):
            sh = s[i] + bias_ref[0, h]
            sk = sinks_ref[h]
            m = jnp.maximum(jnp.max(sh, axis=-1, keepdims=True), sk)
            e = jnp.exp(sh - m)
            denom = jnp.sum(e, axis=-1, keepdims=True) + jnp.exp(sk - m)
            p.append((e / denom).astype(bf16))
        o = [_dot(p[i], v16[kvi[i]]) for i in range(len(pairs))]
        for i, (b, h) in enumerate(pairs):
            o_ref[b, :, h * hd:(h + 1) * hd] = o[i].astype(bf16)


def swa_prompt_attention(qkv, sinks):
    B, L, _ = qkv.shape
    W = WINDOW
    kb = SWA_Q_DIM // SWA_KV_DIM
    prev = lambda n: jnp.maximum(n - 1, 0)
    NB = 1
    return pl.pallas_call(
        functools.partial(_swa_prompt_kernel, NB),
        grid=(B // NB, L // W),
        in_specs=[
            pl.BlockSpec(memory_space=pltpu.SMEM),
            pl.BlockSpec((1, SWA_HEADS, W, 2 * W), lambda b, n: (jnp.minimum(n, 1), 0, 0, 0)),
            pl.BlockSpec((NB, W, SWA_Q_DIM), lambda b, n: (b, n, 0)),
            pl.BlockSpec((NB, W, SWA_KV_DIM), lambda b, n: (b, n, kb)),
            pl.BlockSpec((NB, W, SWA_KV_DIM), lambda b, n: (b, prev(n), kb)),
            pl.BlockSpec((NB, W, SWA_KV_DIM), lambda b, n: (b, n, kb + 1)),
            pl.BlockSpec((NB, W, SWA_KV_DIM), lambda b, n: (b, prev(n), kb + 1)),
        ],
        out_specs=pl.BlockSpec((NB, W, SWA_Q_DIM), lambda b, n: (b, n, 0)),
        out_shape=jax.ShapeDtypeStruct((B, L, SWA_Q_DIM), bf16),
        compiler_params=_params("parallel", "arbitrary"),
        name="swa_prompt",
    )(sinks, _swa_bias_table(), qkv, qkv, qkv, qkv, qkv)


def _swa_sample_kernel(NS, L, R, sinks_ref, qkv_ref, ck_ref, cv_ref, o_ref):
    hd = SWA_HEAD_DIM
    G = SWA_GROUPS
    rows = G * L
    t_c = lax.broadcasted_iota(jnp.int32, (rows, R), 0) % L
    j_c = lax.broadcasted_iota(jnp.int32, (rows, R), 1)
    rel_c = R + t_c - j_c
    valid_c = (rel_c >= 0) & (rel_c <= WINDOW)
    t_n = lax.broadcasted_iota(jnp.int32, (rows, L), 0) % L
    j_n = lax.broadcasted_iota(jnp.int32, (rows, L), 1)
    rel_n = t_n - j_n
    valid_n = (rel_n >= 0) & (rel_n <= WINDOW)
    g_row = lax.broadcasted_iota(jnp.int32, (rows, 1), 0) // L
    bias_c, bias_n, sks = [], [], []
    for kv in range(SWA_KV_HEADS):
        slope = jnp.zeros((rows, 1), f32)
        sk = jnp.zeros((rows, 1), f32)
        for gi in range(G):
            h = kv * G + gi
            slope = jnp.where(g_row == gi, ALIBI_SLOPES[h], slope)
            sk = jnp.where(g_row == gi, sinks_ref[h], sk)
        bias_c.append(slope * rel_c.astype(f32))
        bias_n.append(slope * rel_n.astype(f32))
        sks.append(sk)
    pairs = [(b, kv) for b in range(NS) for kv in range(SWA_KV_HEADS)]
    kcol = lambda kv: slice(SWA_Q_DIM + kv * hd, SWA_Q_DIM + (kv + 1) * hd)
    vcol = lambda kv: slice(SWA_Q_DIM + SWA_KV_DIM + kv * hd, SWA_Q_DIM + SWA_KV_DIM + (kv + 1) * hd)
    q16 = [jnp.concatenate([qkv_ref[b, :, (kv * G + gi) * hd:(kv * G + gi + 1) * hd] for gi in range(G)],
                           axis=0).astype(bf16) for b, kv in pairs]
    s_c = [_dot_nt(q16[i], ck_ref[b, :, kv * hd:(kv + 1) * hd].astype(bf16)) for i, (b, kv) in enumerate(pairs)]
    s_n = [_dot_nt(q16[i], qkv_ref[b, :, kcol(kv)].astype(bf16)) for i, (b, kv) in enumerate(pairs)]
    p_c, p_n = [], []
    for i, (b, kv) in enumerate(pairs):
        sc = jnp.where(valid_c, s_c[i] * SWA_SCALE - bias_c[kv], -jnp.inf)
        sn = jnp.where(valid_n, s_n[i] * SWA_SCALE - bias_n[kv], -jnp.inf)
        m = jnp.maximum(jnp.maximum(jnp.max(sc, axis=-1, keepdims=True),
                                    jnp.max(sn, axis=-1, keepdims=True)), sks[kv])
        ec = jnp.exp(sc - m)
        en = jnp.exp(sn - m)
        denom = jnp.sum(ec, axis=-1, keepdims=True) + jnp.sum(en, axis=-1, keepdims=True) + jnp.exp(sks[kv] - m)
        p_c.append((ec / denom).astype(bf16))
        p_n.append((en / denom).astype(bf16))
    o = [_dot(p_c[i], cv_ref[b, :, kv * hd:(kv + 1) * hd].astype(bf16))
         + _dot(p_n[i], qkv_ref[b, :, vcol(kv)].astype(bf16)) for i, (b, kv) in enumerate(pairs)]
    for i, (b, kv) in enumerate(pairs):
        for gi in range(G):
            h = kv * G + gi
            o_ref[b, :, h * hd:(h + 1) * hd] = o[i][gi * L:(gi + 1) * L].astype(bf16)


def swa_sample_attention(qkv, cache_k, cache_v, sinks):
    B, L, _ = qkv.shape
    R = cache_k.shape[1]
    NS = min(B, SWA_SAMPLE_SEQS)
    return pl.pallas_call(
        functools.partial(_swa_sample_kernel, NS, L, R),
        grid=(B // NS,),
        in_specs=[
            pl.BlockSpec(memory_space=pltpu.SMEM),
            pl.BlockSpec((NS, L, SWA_QKV_DIM), lambda b: (b, 0, 0)),
            pl.BlockSpec((NS, R, SWA_KV_DIM), lambda b: (b, 0, 0)),
            pl.BlockSpec((NS, R, SWA_KV_DIM), lambda b: (b, 0, 0)),
        ],
        out_specs=pl.BlockSpec((NS, L, SWA_Q_DIM), lambda b: (b, 0, 0)),
        out_shape=jax.ShapeDtypeStruct((B, L, SWA_Q_DIM), bf16),
        compiler_params=_params("parallel"),
        name="swa_sample",
    )(sinks, qkv, cache_k, cache_v)


def kernel(x_prompt, x_sample, state_conv, state_delta, cache_k, cache_v, norm_mix, norm_ffn, norm_final,
           dn_w_in, dn_conv_w, dn_a_log, dn_dt_bias, dn_norm_w, dn_w_out, swa_w_qkv, swa_b_qkv, swa_sinks,
           swa_w_o, swa_b_o, ffn_w_gu, ffn_w_down):
    Bp, Lp, D = x_prompt.shape
    Bs, Ls, _ = x_sample.shape
    depth = norm_mix.shape[0]
    n_delta = dn_w_in.shape[0]
    yp = x_prompt.reshape(Bp * Lp, D)
    ys = x_sample.reshape(Bs * Ls, D)
    R = cache_k.shape[2]

    w_in = jnp.pad(dn_w_in.astype(bf16), ((0, 0), (0, 0), (0, DN_IN_PAD - DN_IN_DIM)))
    w_zba = w_in[:, :, DN_CONV_DIM:]
    w_out = dn_w_out.astype(bf16)
    w_qkv = swa_w_qkv.astype(bf16)
    w_o = swa_w_o.astype(bf16)
    w_gu = ffn_w_gu.astype(bf16)
    w_down = ffn_w_down.astype(bf16)
    zero_b = jnp.zeros((D,), f32)
    zero_in = jnp.zeros((DN_IN_PAD,), f32)
    zero_zba = jnp.zeros((DN_ZBA_PAD,), f32)

    kp_list, vp_list, ks_list, vs_list = [], [], [], []
    prompt_state = None
    sample_states = None
    conv_p = []
    for i in range(depth):
        j = i // 2
        if i % 2 == 0:
            wts = (dn_conv_w[j], dn_a_log[j], dn_dt_bias[j], dn_norm_w[j])
            qkv, cb = proj_conv(yp, norm_mix[i], w_in, j, dn_conv_w[j], Lp)
            zba = rms_matmul(yp, norm_mix[i], w_zba, j, zero_zba, DN_ZBA_PAD)
            o, sn = delta_mixer((qkv.reshape(Bp, Lp, DN_CONV_DIM), zba.reshape(Bp, Lp, DN_ZBA_PAD)), None,
                                prompt_state, j, n_delta, *wts, C=min(DN_CHUNK, Lp), NS=1)
            prompt_state = (sn,)
            conv_p.append(cb)
            yp = matmul_resid(o.reshape(Bp * Lp, DN_VAL_DIM), w_out, j, zero_b, yp)
            proj = rms_matmul(ys, norm_mix[i], w_in, j, zero_in, 896).reshape(Bs, Ls, DN_IN_PAD)
            o, cb, sn = delta_mixer(proj, (state_conv, state_delta), sample_states, j, n_delta, *wts, C=Ls,
                                    NS=min(Bs, DN_SAMPLE_SEQS))
            sample_states = (cb, sn)
            ys = matmul_resid(o.reshape(Bs * Ls, DN_VAL_DIM), w_out, j, zero_b, ys)
        else:
            qkv = rms_matmul(yp, norm_mix[i], w_qkv, j, swa_b_qkv[j], 768).reshape(Bp, Lp, SWA_QKV_DIM)
            o = swa_prompt_attention(qkv, swa_sinks[j])
            yp = matmul_resid(o.reshape(Bp * Lp, SWA_Q_DIM), w_o, j, swa_b_o[j], yp)
            Rp = min(WINDOW, Lp)
            kp_list.append(qkv[:, Lp - Rp:, SWA_Q_DIM:SWA_Q_DIM + SWA_KV_DIM].reshape(Bp, Rp, SWA_KV_HEADS, SWA_HEAD_DIM))
            vp_list.append(qkv[:, Lp - Rp:, SWA_Q_DIM + SWA_KV_DIM:].reshape(Bp, Rp, SWA_KV_HEADS, SWA_HEAD_DIM))
            qkv = rms_matmul(ys, norm_mix[i], w_qkv, j, swa_b_qkv[j], 768).reshape(Bs, Ls, SWA_QKV_DIM)
            o = swa_sample_attention(qkv, cache_k[j].reshape(Bs, R, SWA_KV_DIM), cache_v[j].reshape(Bs, R, SWA_KV_DIM),
                                     swa_sinks[j])
            ys = matmul_resid(o.reshape(Bs * Ls, SWA_Q_DIM), w_o, j, swa_b_o[j], ys)
            ks_list.append(qkv[:, :, SWA_Q_DIM:SWA_Q_DIM + SWA_KV_DIM].reshape(Bs, Ls, SWA_KV_HEADS, SWA_HEAD_DIM))
            vs_list.append(qkv[:, :, SWA_Q_DIM + SWA_KV_DIM:].reshape(Bs, Ls, SWA_KV_HEADS, SWA_HEAD_DIM))
        last = i == depth - 1
        yp = ffn(yp, norm_ffn[i], w_gu, w_down, i, norm_final, last)
        ys = ffn(ys, norm_ffn[i], w_gu, w_down, i, norm_final, last)

    return (yp.reshape(Bp, Lp, D), ys.reshape(Bs, Ls, D),
            jnp.stack(conv_p), prompt_state[0], jnp.stack(kp_list), jnp.stack(vp_list),
            sample_states[0], sample_states[1], jnp.stack(ks_list), jnp.stack(vs_list))
```

```python
import functools

import jax
import jax.numpy as jnp
from jax import lax
from jax.experimental import pallas as pl
from jax.experimental.pallas import tpu as pltpu

f32 = jnp.float32
bf16 = jnp.bfloat16

D_MODEL = 1024
DN_QK_HEADS = 8
DN_V_HEADS = 16
DN_HEAD = 128
DN_KEY_DIM = DN_QK_HEADS * DN_HEAD
DN_VAL_DIM = DN_V_HEADS * DN_HEAD
DN_CONV_DIM = 2 * DN_KEY_DIM + DN_VAL_DIM
DN_CONV = 4
DN_IN_DIM = DN_CONV_DIM + DN_VAL_DIM + 2 * DN_V_HEADS
DN_IN_PAD = 6272
DN_ZBA_PAD = DN_IN_PAD - DN_CONV_DIM
SWA_HEADS = 16
SWA_KV_HEADS = 4
SWA_GROUPS = SWA_HEADS // SWA_KV_HEADS
SWA_HEAD_DIM = 64
SWA_Q_DIM = SWA_HEADS * SWA_HEAD_DIM
SWA_KV_DIM = SWA_KV_HEADS * SWA_HEAD_DIM
SWA_QKV_DIM = SWA_Q_DIM + 2 * SWA_KV_DIM
SWA_SCALE = SWA_HEAD_DIM ** -0.5
WINDOW = 128
D_FF = 2816
RMS_EPS = 1e-6
L2_EPS = 1e-6
ALIBI_SLOPES = tuple(float(2.0 ** (-8.0 * (i + 1) / SWA_HEADS)) for i in range(SWA_HEADS))

VMEM_LIMIT_BYTES = 56 * 1024 * 1024
DN_CHUNK = 64
SWA_SAMPLE_SEQS = 8
DN_SAMPLE_SEQS = 4
SWA_HEAD_GROUP = 16


def _params(*sem):
    return pltpu.CompilerParams(dimension_semantics=sem, vmem_limit_bytes=VMEM_LIMIT_BYTES)


def _sigmoid(x):
    return 1.0 / (1.0 + jnp.exp(-x))


def _silu(x):
    h = 0.5 * x
    return h + h * jnp.tanh(h)


def _rms(x, w):
    return x * lax.rsqrt(jnp.mean(x * x, axis=-1, keepdims=True) + RMS_EPS) * w


def _dot(a, b):
    return jnp.dot(a, b, preferred_element_type=f32)


def _dot_nt(a, b):
    return lax.dot_general(a, b, (((1,), (1,)), ((), ())), preferred_element_type=f32)


def _dot_tn(a, b):
    return lax.dot_general(a, b, (((0,), (0,)), ((), ())), preferred_element_type=f32)


def _rms_matmul_kernel(x_ref, nw_ref, w_ref, b_ref, o_ref, xn_ref):
    @pl.when(pl.program_id(1) == 0)
    def _():
        xn_ref[...] = _rms(x_ref[...], nw_ref[...]).astype(bf16)

    o_ref[...] = _dot(xn_ref[...], w_ref[...]) + b_ref[...]


def rms_matmul(x, nw, w_stack, layer, b, tn):
    m, d = x.shape
    n = w_stack.shape[2]
    tm = min(m, 1024)
    return pl.pallas_call(
        _rms_matmul_kernel,
        grid=(m // tm, n // tn),
        in_specs=[
            pl.BlockSpec((tm, d), lambda i, j: (i, 0)),
            pl.BlockSpec((1, d), lambda i, j: (0, 0)),
            pl.BlockSpec((None, d, tn), lambda i, j: (layer, 0, j)),
            pl.BlockSpec((1, tn), lambda i, j: (0, j)),
        ],
        out_specs=pl.BlockSpec((tm, tn), lambda i, j: (i, j)),
        out_shape=jax.ShapeDtypeStruct((m, n), f32),
        scratch_shapes=[pltpu.VMEM((tm, d), bf16)],
        compiler_params=_params("parallel", "arbitrary"),
        name="rms_matmul",
    )(x, nw.reshape(1, d), w_stack, b.reshape(1, n))


def _proj_conv_kernel(tiles_per_seq, RS, x_ref, nw_ref, w_ref, cw_ref, o_ref, cs_ref, xn_ref, ext_ref, carry_ref):
    i = pl.program_id(0)
    j = pl.program_id(1)
    H = DN_HEAD
    tm, tn = o_ref.shape

    @pl.when(j == 0)
    def _():
        xn_ref[...] = _rms(x_ref[...], nw_ref[...]).astype(bf16)

    RC = 128

    def body(kind):
        ext_ref[0:8, :] = jnp.where(i % tiles_per_seq == 0, 0.0, carry_ref[j])
        for r in range(tm // RS):
            ext_ref[8:8 + RS, :] = _dot(xn_ref[r * RS:(r + 1) * RS, :], w_ref[...])
            for c in range(RS // RC):
                for hh in range(tn // H):
                    cols = slice(hh * H, (hh + 1) * H)
                    cw = cw_ref[:, cols]
                    e = ext_ref[c * RC:c * RC + RC + 8, cols]
                    y = e[8:] * cw[3:4]
                    for k in range(DN_CONV - 1):
                        y = y + pltpu.roll(e, DN_CONV - 1 - k, 0)[8:] * cw[k:k + 1]
                    y = _silu(y)
                    if kind < 2:
                        inv = lax.rsqrt(jnp.sum(y * y, axis=-1, keepdims=True) + L2_EPS)
                        y = y * (inv * (H ** -0.5) if kind == 0 else inv)
                    o_ref[r * RS + c * RC:r * RS + (c + 1) * RC, cols] = y
            ext_ref[0:8, :] = ext_ref[RS:RS + 8, :]
        carry_ref[j] = ext_ref[0:8, :]
        cs_ref[0] = ext_ref[5:8, :]

    pl.when(j == 0)(functools.partial(body, 0))
    pl.when(j == 1)(functools.partial(body, 1))
    pl.when(j >= 2)(functools.partial(body, 2))


def proj_conv(x, nw, w_stack, layer, conv_w, seq_len):
    m, d = x.shape
    tm, tn, RS = 1024, DN_KEY_DIM, 512
    assert seq_len % tm == 0 and m % seq_len == 0
    tiles_per_seq = seq_len // tm
    nj = DN_CONV_DIM // tn
    qkv, tails = pl.pallas_call(
        functools.partial(_proj_conv_kernel, tiles_per_seq, RS),
        grid=(m // tm, nj),
        in_specs=[
            pl.BlockSpec((tm, d), lambda i, j: (i, 0)),
            pl.BlockSpec((1, d), lambda i, j: (0, 0)),
            pl.BlockSpec((None, d, tn), lambda i, j: (layer, 0, j)),
            pl.BlockSpec((DN_CONV, tn), lambda i, j: (0, j)),
        ],
        out_specs=[
            pl.BlockSpec((tm, tn), lambda i, j: (i, j)),
            pl.BlockSpec((1, DN_CONV - 1, tn), lambda i, j: (i, 0, j)),
        ],
        out_shape=[
            jax.ShapeDtypeStruct((m, DN_CONV_DIM), f32),
            jax.ShapeDtypeStruct((m // tm, DN_CONV - 1, DN_CONV_DIM), f32),
        ],
        scratch_shapes=[
            pltpu.VMEM((tm, d), bf16),
            pltpu.VMEM((RS + 8, tn), f32),
            pltpu.VMEM((nj, 8, tn), f32),
        ],
        compiler_params=_params("arbitrary", "arbitrary"),
        name="proj_conv",
    )(x, nw.reshape(1, d), w_stack, conv_w)
    return qkv, tails.reshape(m // seq_len, tiles_per_seq, DN_CONV - 1, DN_CONV_DIM)[:, -1]


def _matmul_resid_kernel(a_ref, w_ref, b_ref, r_ref, o_ref):
    o_ref[...] = r_ref[...] + _dot(a_ref[...], w_ref[...]) + b_ref[...]


def matmul_resid(a, w_stack, layer, b, resid):
    m, k = a.shape
    n = w_stack.shape[2]
    tm = min(m, 1024)
    return pl.pallas_call(
        _matmul_resid_kernel,
        grid=(m // tm,),
        in_specs=[
            pl.BlockSpec((tm, k), lambda i: (i, 0)),
            pl.BlockSpec((None, k, n), lambda i: (layer, 0, 0)),
            pl.BlockSpec((1, n), lambda i: (0, 0)),
            pl.BlockSpec((tm, n), lambda i: (i, 0)),
        ],
        out_specs=pl.BlockSpec((tm, n), lambda i: (i, 0)),
        out_shape=jax.ShapeDtypeStruct((m, n), f32),
        compiler_params=_params("parallel"),
        name="matmul_resid",
    )(a, w_stack, b.reshape(1, n), resid)


def _ffn_kernel(final_norm, x_ref, nw_ref, wg_ref, wu_ref, wd_ref, nf_ref, o_ref, xn_ref, acc_ref):
    j = pl.program_id(1)

    @pl.when(j == 0)
    def _():
        xn_ref[...] = _rms(x_ref[...], nw_ref[...]).astype(bf16)
        acc_ref[...] = jnp.zeros_like(acc_ref)

    xn = xn_ref[...]
    tf = wg_ref.shape[1]
    half = (tf // 256) * 128
    for c0, c1 in ((0, half), (half, tf)):
        g = _dot(xn, wg_ref[:, c0:c1])
        u = _dot(xn, wu_ref[:, c0:c1])
        h = (_silu(g) * u).astype(bf16)
        acc_ref[...] += _dot(h, wd_ref[c0:c1, :])

    @pl.when(j == pl.num_programs(1) - 1)
    def _():
        y = x_ref[...] + acc_ref[...]
        if final_norm:
            y = _rms(y, nf_ref[...])
        o_ref[...] = y


def ffn(x, nw, wgu_stack, wd_stack, layer, nf, final_norm):
    m, d = x.shape
    tm = min(m, 1024)
    tf = D_FF // 2
    nj = D_FF // tf
    return pl.pallas_call(
        functools.partial(_ffn_kernel, final_norm),
        grid=(m // tm, nj),
        in_specs=[
            pl.BlockSpec((tm, d), lambda i, j: (i, 0)),
            pl.BlockSpec((1, d), lambda i, j: (0, 0)),
            pl.BlockSpec((None, d, tf), lambda i, j: (layer, 0, j)),
            pl.BlockSpec((None, d, tf), lambda i, j: (layer, 0, j + nj)),
            pl.BlockSpec((None, tf, d), lambda i, j: (layer, j, 0)),
            pl.BlockSpec((1, d), lambda i, j: (0, 0)),
        ],
        out_specs=pl.BlockSpec((tm, d), lambda i, j: (i, 0)),
        out_shape=jax.ShapeDtypeStruct((m, d), f32),
        scratch_shapes=[pltpu.VMEM((tm, d), bf16), pltpu.VMEM((tm, d), f32)],
        compiler_params=_params("parallel", "arbitrary"),
        name="ffn",
    )(x, nw.reshape(1, d), wgu_stack, wgu_stack, wd_stack, nf.reshape(1, d))


def _split3(x):
    hi = x.astype(bf16)
    r1 = x - hi.astype(f32)
    mid = r1.astype(bf16)
    lo = (r1 - mid.astype(f32)).astype(bf16)
    return hi, mid, lo


def _delta_kernel(C, NS, n_iter, has_init, n_alias, pre_conv, *refs):
    if pre_conv:
        qkv_ref, zba_ref = refs[0:2]
        pos = 2
    else:
        proj_ref = refs[0]
        pos = 1
    if has_init:
        conv0_ref, s0_ref = refs[pos:pos + 2]
        pos += 2
    cw_ref, alog_ref, dtb_ref, nw_ref = refs[pos:pos + 4]
    pos += 4 + n_alias
    if pre_conv:
        o_ref, sout_ref = refs[pos:pos + 2]
    else:
        o_ref, convout_ref, sout_ref = refs[pos:pos + 3]
        full_ref, qn_ref, kn_ref = refs[pos + 3:]
    n = pl.program_id(1)
    H = DN_HEAD
    NH = DN_V_HEADS
    NQ = DN_QK_HEADS
    per = NH // NQ
    seqs = range(NS)
    pairs = [(s, h) for s in seqs for h in range(NH)]
    qkp = [(s, qh) for s in seqs for qh in range(NQ)]
    P = range(len(pairs))
    qi = [s * NQ + h // per for s, h in pairs]

    @pl.when(n == 0)
    def _():
        if has_init:
            full_ref[:, 5:8, :] = conv0_ref[...]
            sout_ref[...] = s0_ref[...]
        else:
            if not pre_conv:
                full_ref[:, 5:8, :] = jnp.zeros((NS, DN_CONV - 1, DN_CONV_DIM), f32)
            sout_ref[...] = jnp.zeros_like(sout_ref)

    if pre_conv:
        get_ba = lambda s: zba_ref[s, :, DN_VAL_DIM:DN_VAL_DIM + H]
        get_z = lambda s, h: zba_ref[s, :, h * H:(h + 1) * H]
    else:
        get_ba = lambda s: proj_ref[s, :, DN_CONV_DIM + DN_VAL_DIM:DN_IN_PAD]
        get_z = lambda s, h: proj_ref[s, :, DN_CONV_DIM + h * H:DN_CONV_DIM + (h + 1) * H]
        for s in seqs:
            full_ref[s, 8:8 + C, :] = proj_ref[s, :, 0:DN_CONV_DIM]

    def conv_silu(s, cb):
        cols = slice(cb * H, (cb + 1) * H)
        w = cw_ref[:, cols]
        y = (full_ref[s, 5:5 + C, cols] * w[0:1] + full_ref[s, 6:6 + C, cols] * w[1:2]
             + full_ref[s, 7:7 + C, cols] * w[2:3] + full_ref[s, 8:8 + C, cols] * w[3:4])
        return _silu(y)

    if not pre_conv:
        for s in seqs:
            for cb in range(2 * NQ):
                y = conv_silu(s, cb)
                inv = lax.rsqrt(jnp.sum(y * y, axis=-1, keepdims=True) + L2_EPS)
                if cb < NQ:
                    qn_ref[s * NQ + cb] = y * (inv * (H ** -0.5))
                else:
                    kn_ref[s * NQ + cb - NQ] = y * inv

    ri = lax.broadcasted_iota(jnp.int32, (C, C), 0)
    ci = lax.broadcasted_iota(jnp.int32, (C, C), 1)
    causal = ri >= ci
    strict = ri > ci
    eye = jnp.where(ri == ci, 1.0, 0.0).astype(f32)
    tri16 = jnp.where(causal, 1.0, 0.0).astype(bf16)
    beta, G, GT = [], [], []
    for s in seqs:
        ba = get_ba(s)
        beta.append(_sigmoid(ba[:, 0:NH]))
        xa = ba[:, NH:2 * NH] + dtb_ref[...]
        softplus = jnp.maximum(xa, 0.0) + jnp.log(1.0 + jnp.exp(-jnp.abs(xa)))
        g = -jnp.exp(alog_ref[...]) * softplus
        g3 = jnp.concatenate(_split3(g), axis=1)
        G3 = _dot(tri16, g3)
        G.append(G3[:, 0:NH] + G3[:, NH:2 * NH] + G3[:, 2 * NH:3 * NH])
        GT3 = lax.dot_general(g3, tri16, (((0,), (1,)), ((), ())), preferred_element_type=f32)
        GT.append(GT3[0:NH] + GT3[NH:2 * NH] + GT3[2 * NH:3 * NH])

    if pre_conv:
        qs = [qkv_ref[s, :, qh * H:(qh + 1) * H] for s, qh in qkp]
        ks = [qkv_ref[s, :, DN_KEY_DIM + qh * H:DN_KEY_DIM + (qh + 1) * H] for s, qh in qkp]
    else:
        ks = [kn_ref[i] for i in range(len(qkp))]
        qs = [qn_ref[i] for i in range(len(qkp))]
    kqs = [_dot_nt(jnp.concatenate([ks[i], qs[i]], axis=0).astype(bf16), ks[i].astype(bf16))
           for i in range(len(qkp))]
    kTs = [ks[i].T for i in range(len(qkp))]
    Gc = [G[s][:, h:h + 1] for s, h in pairs]
    Gr = [GT[s][h:h + 1, :] for s, h in pairs]
    bc = [beta[s][:, h:h + 1] for s, h in pairs]
    decay = [jnp.where(causal, jnp.exp(jnp.where(causal, Gc[p] - Gr[p], 0.0)), 0.0) for p in P]
    M = [jnp.where(strict, -(bc[p] * kqs[qi[p]][:C] * decay[p]), 0.0) for p in P]
    T = [eye + M[p] for p in P]
    Mp = [_dot(M[p].astype(bf16), M[p].astype(bf16)) for p in P]
    for it in range(n_iter):
        if it < n_iter - 1:
            tm = [_dot(jnp.concatenate([T[p], Mp[p]], axis=0).astype(bf16), Mp[p].astype(bf16)) for p in P]
            T = [T[p] + tm[p][:C] for p in P]
            Mp = [tm[p][C:] for p in P]
        else:
            tm = [_dot(T[p].astype(bf16), Mp[p].astype(bf16)) for p in P]
            T = [T[p] + tm[p] for p in P]
    if pre_conv:
        vs = [qkv_ref[s, :, 2 * DN_KEY_DIM + h * H:2 * DN_KEY_DIM + (h + 1) * H] for s, h in pairs]
    else:
        vs = [conv_silu(s, 2 * NQ + h) for s, h in pairs]
        for s in seqs:
            tail = full_ref[s, C + 5:C + 8, :]
            full_ref[s, 5:8, :] = tail
            convout_ref[s] = tail

    eG = [jnp.exp(Gc[p]) for p in P]
    uw = [_dot(T[p].astype(bf16),
               jnp.concatenate([vs[p] * bc[p], ks[qi[p]] * (bc[p] * eG[p])], axis=1).astype(bf16))
          for p in P]
    S = [sout_ref[s, h] for s, h in pairs]
    ws = [_dot(jnp.concatenate([uw[p][:, H:], qs[qi[p]] * eG[p]], axis=0).astype(bf16), S[p].astype(bf16))
          for p in P]
    unew = [uw[p][:, :H] - ws[p][:C] for p in P]
    Gl = [Gc[p][C - 1:C, :] for p in P]
    r = [_dot(jnp.concatenate([kqs[qi[p]][C:] * decay[p], kTs[qi[p]] * jnp.exp(Gl[p] - Gr[p])],
                              axis=0).astype(bf16), unew[p].astype(bf16))
         for p in P]
    for p, (s, h) in enumerate(pairs):
        sout_ref[s, h] = S[p] * jnp.exp(Gl[p]) + r[p][C:]
        on = _rms(ws[p][C:] + r[p][:C], nw_ref[...])
        o_ref[s, :, h * H:(h + 1) * H] = (on * _silu(get_z(s, h))).astype(bf16)


def delta_mixer(proj, init, prev_out, layer, n_layers, conv_w, a_log, dt_bias, norm_w, C, NS):
    pre_conv = isinstance(proj, tuple)
    assert not (pre_conv and init is not None)
    B, L, _ = proj[0].shape if pre_conv else proj.shape
    n_iter = max(1, (C - 1).bit_length()) - 1
    H = DN_HEAD
    has_init = init is not None
    n_alias = 0 if prev_out is None else len(prev_out)
    conv_spec = pl.BlockSpec((None, NS, DN_CONV - 1, DN_CONV_DIM), lambda b, n: (layer, b, 0, 0))
    state_spec = pl.BlockSpec((None, NS, DN_V_HEADS, H, H), lambda b, n: (layer, b, 0, 0, 0))
    small = lambda shape: pl.BlockSpec(shape, lambda b, n: (0,) * len(shape))
    if pre_conv:
        in_specs = [pl.BlockSpec((NS, C, DN_CONV_DIM), lambda b, n: (b, n, 0)),
                    pl.BlockSpec((NS, C, DN_ZBA_PAD), lambda b, n: (b, n, 0))]
        args = list(proj)
    else:
        in_specs = [pl.BlockSpec((NS, C, DN_IN_PAD), lambda b, n: (b, n, 0))]
        args = [proj]
    if has_init:
        in_specs += [conv_spec, state_spec]
        args += list(init)
    in_specs += [small((DN_CONV, DN_CONV_DIM)), small((1, DN_V_HEADS)), small((1, DN_V_HEADS)), small((1, H))]
    args += [conv_w, a_log.reshape(1, -1), dt_bias.reshape(1, -1), norm_w.reshape(1, -1)]
    out_specs = [pl.BlockSpec((NS, C, DN_VAL_DIM), lambda b, n: (b, n, 0)), conv_spec, state_spec]
    out_shape = [
        jax.ShapeDtypeStruct((B, L, DN_VAL_DIM), bf16),
        jax.ShapeDtypeStruct((n_layers, B, DN_CONV - 1, DN_CONV_DIM), f32),
        jax.ShapeDtypeStruct((n_layers, B, DN_V_HEADS, H, H), f32),
    ]
    scratch_shapes = [
        pltpu.VMEM((NS, C + 8, DN_CONV_DIM), f32),
        pltpu.VMEM((NS * DN_QK_HEADS, C, H), f32),
        pltpu.VMEM((NS * DN_QK_HEADS, C, H), f32),
    ]
    if pre_conv:
        del out_specs[1], out_shape[1]
        scratch_shapes = []
    aliases = {}
    if prev_out is not None:
        first_out = len(out_specs) - len(prev_out)
        aliases = {len(args) + k: first_out + k for k in range(len(prev_out))}
        in_specs += [pl.BlockSpec(memory_space=pl.ANY)] * len(prev_out)
        args += list(prev_out)
    return pl.pallas_call(
        functools.partial(_delta_kernel, C, NS, n_iter, has_init, n_alias, pre_conv),
        grid=(B // NS, L // C),
        in_specs=in_specs,
        out_specs=out_specs,
        out_shape=out_shape,
        scratch_shapes=scratch_shapes,
        input_output_aliases=aliases,
        compiler_params=_params("parallel", "arbitrary"),
        name="delta_mixer",
    )(*args)


def _swa_bias_table():
    W = WINDOW
    ri = lax.broadcasted_iota(jnp.int32, (W, 2 * W), 0)
    ci = lax.broadcasted_iota(jnp.int32, (W, 2 * W), 1)
    rel = W + ri - ci
    band = (rel >= 0) & (rel <= WINDOW)
    slopes = jnp.asarray(ALIBI_SLOPES, f32).reshape(SWA_HEADS, 1, 1)
    alibi = -slopes * rel.astype(f32)[None]
    later = jnp.where(band[None], alibi, -jnp.inf)
    first = jnp.where((band & (ci >= W))[None], alibi, -jnp.inf)
    return jnp.stack([first, later])


def _swa_prompt_kernel(NB, sinks_ref, bias_ref, q_ref, kc_ref, kp_ref, vc_ref, vp_ref, o_ref):
    hd = SWA_HEAD_DIM
    k16, v16 = [], []
    for b in range(NB):
        for kv in range(SWA_KV_HEADS):
            cs = slice(kv * hd, (kv + 1) * hd)
            k16.append(jnp.concatenate([kp_ref[b, :, cs], kc_ref[b, :, cs]], axis=0).astype(bf16))
            v16.append(jnp.concatenate([vp_ref[b, :, cs], vc_ref[b, :, cs]], axis=0).astype(bf16))
    all_pairs = [(b, h) for b in range(NB) for h in range(SWA_HEADS)]
    for g0 in range(0, len(all_pairs), SWA_HEAD_GROUP):
        pairs = all_pairs[g0:g0 + SWA_HEAD_GROUP]
        kvi = [b * SWA_KV_HEADS + h // SWA_GROUPS for b, h in pairs]
        s = [_dot_nt((q_ref[b, :, h * hd:(h + 1) * hd] * SWA_SCALE).astype(bf16), k16[kvi[i]])
             for i, (b, h) in enumerate(pairs)]
        p = []
        for i, (b, h) in enumerate(pairs):
            sh = s[i] + bias_ref[0, h]
            sk = sinks_ref[h]
            m = jnp.maximum(jnp.max(sh, axis=-1, keepdims=True), sk)
            e = jnp.exp(sh - m)
            denom = jnp.sum(e, axis=-1, keepdims=True) + jnp.exp(sk - m)
            p.append((e / denom).astype(bf16))
        o = [_dot(p[i], v16[kvi[i]]) for i in range(len(pairs))]
        for i, (b, h) in enumerate(pairs):
            o_ref[b, :, h * hd:(h + 1) * hd] = o[i].astype(bf16)


def swa_prompt_attention(qkv, sinks):
    B, L, _ = qkv.shape
    W = WINDOW
    kb = SWA_Q_DIM // SWA_KV_DIM
    prev = lambda n: jnp.maximum(n - 1, 0)
    NB = 1
    return pl.pallas_call(
        functools.partial(_swa_prompt_kernel, NB),
        grid=(B // NB, L // W),
        in_specs=[
            pl.BlockSpec(memory_space=pltpu.SMEM),
            pl.BlockSpec((1, SWA_HEADS, W, 2 * W), lambda b, n: (jnp.minimum(n, 1), 0, 0, 0)),
            pl.BlockSpec((NB, W, SWA_Q_DIM), lambda b, n: (b, n, 0)),
            pl.BlockSpec((NB, W, SWA_KV_DIM), lambda b, n: (b, n, kb)),
            pl.BlockSpec((NB, W, SWA_KV_DIM), lambda b, n: (b, prev(n), kb)),
            pl.BlockSpec((NB, W, SWA_KV_DIM), lambda b, n: (b, n, kb + 1)),
            pl.BlockSpec((NB, W, SWA_KV_DIM), lambda b, n: (b, prev(n), kb + 1)),
        ],
        out_specs=pl.BlockSpec((NB, W, SWA_Q_DIM), lambda b, n: (b, n, 0)),
        out_shape=jax.ShapeDtypeStruct((B, L, SWA_Q_DIM), bf16),
        compiler_params=_params("parallel", "arbitrary"),
        name="swa_prompt",
    )(sinks, _swa_bias_table(), qkv, qkv, qkv, qkv, qkv)


def _swa_sample_kernel(NS, L, R, sinks_ref, qkv_ref, ck_ref, cv_ref, o_ref):
    hd = SWA_HEAD_DIM
    G = SWA_GROUPS
    rows = G * L
    t_c = lax.broadcasted_iota(jnp.int32, (rows, R), 0) % L
    j_c = lax.broadcasted_iota(jnp.int32, (rows, R), 1)
    rel_c = R + t_c - j_c
    valid_c = (rel_c >= 0) & (rel_c <= WINDOW)
    t_n = lax.broadcasted_iota(jnp.int32, (rows, L), 0) % L
    j_n = lax.broadcasted_iota(jnp.int32, (rows, L), 1)
    rel_n = t_n - j_n
    valid_n = (rel_n >= 0) & (rel_n <= WINDOW)
    g_row = lax.broadcasted_iota(jnp.int32, (rows, 1), 0) // L
    bias_c, bias_n, sks = [], [], []
    for kv in range(SWA_KV_HEADS):
        slope = jnp.zeros((rows, 1), f32)
        sk = jnp.zeros((rows, 1), f32)
        for gi in range(G):
            h = kv * G + gi
            slope = jnp.where(g_row == gi, ALIBI_SLOPES[h], slope)
            sk = jnp.where(g_row == gi, sinks_ref[h], sk)
        bias_c.append(slope * rel_c.astype(f32))
        bias_n.append(slope * rel_n.astype(f32))
        sks.append(sk)
    pairs = [(b, kv) for b in range(NS) for kv in range(SWA_KV_HEADS)]
    kcol = lambda kv: slice(SWA_Q_DIM + kv * hd, SWA_Q_DIM + (kv + 1) * hd)
    vcol = lambda kv: slice(SWA_Q_DIM + SWA_KV_DIM + kv * hd, SWA_Q_DIM + SWA_KV_DIM + (kv + 1) * hd)
    q16 = [jnp.concatenate([qkv_ref[b, :, (kv * G + gi) * hd:(kv * G + gi + 1) * hd] for gi in range(G)],
                           axis=0).astype(bf16) for b, kv in pairs]
    s_c = [_dot_nt(q16[i], ck_ref[b, :, kv * hd:(kv + 1) * hd].astype(bf16)) for i, (b, kv) in enumerate(pairs)]
    s_n = [_dot_nt(q16[i], qkv_ref[b, :, kcol(kv)].astype(bf16)) for i, (b, kv) in enumerate(pairs)]
    p_c, p_n = [], []
    for i, (b, kv) in enumerate(pairs):
        sc = jnp.where(valid_c, s_c[i] * SWA_SCALE - bias_c[kv], -jnp.inf)
        sn = jnp.where(valid_n, s_n[i] * SWA_SCALE - bias_n[kv], -jnp.inf)
        m = jnp.maximum(jnp.maximum(jnp.max(sc, axis=-1, keepdims=True),
                                    jnp.max(sn, axis=-1, keepdims=True)), sks[kv])
        ec = jnp.exp(sc - m)
        en = jnp.exp(sn - m)
        denom = jnp.sum(ec, axis=-1, keepdims=True) + jnp.sum(en, axis=-1, keepdims=True) + jnp.exp(sks[kv] - m)
        p_c.append((ec / denom).astype(bf16))
        p_n.append((en / denom).astype(bf16))
    o = [_dot(p_c[i], cv_ref[b, :, kv * hd:(kv + 1) * hd].astype(bf16))
         + _dot(p_n[i], qkv_ref[b, :, vcol(kv)].astype(bf16)) for i, (b, kv) in enumerate(pairs)]
    for i, (b, kv) in enumerate(pairs):
        for gi in range(G):
            h = kv * G + gi
            o_ref[b, :, h * hd:(h + 1) * hd] = o[i][gi * L:(gi + 1) * L].astype(bf16)


def swa_sample_attention(qkv, cache_k, cache_v, sinks):
    B, L, _ = qkv.shape
    R = cache_k.shape[1]
    NS = min(B, SWA_SAMPLE_SEQS)
    return pl.pallas_call(
        functools.partial(_swa_sample_kernel, NS, L, R),
        grid=(B // NS,),
        in_specs=[
            pl.BlockSpec(memory_space=pltpu.SMEM),
            pl.BlockSpec((NS, L, SWA_QKV_DIM), lambda b: (b, 0, 0)),
            pl.BlockSpec((NS, R, SWA_KV_DIM), lambda b: (b, 0, 0)),
            pl.BlockSpec((NS, R, SWA_KV_DIM), lambda b: (b, 0, 0)),
        ],
        out_specs=pl.BlockSpec((NS, L, SWA_Q_DIM), lambda b: (b, 0, 0)),
        out_shape=jax.ShapeDtypeStruct((B, L, SWA_Q_DIM), bf16),
        compiler_params=_params("parallel"),
        name="swa_sample",
    )(sinks, qkv, cache_k, cache_v)


def kernel(x_prompt, x_sample, state_conv, state_delta, cache_k, cache_v, norm_mix, norm_ffn, norm_final,
           dn_w_in, dn_conv_w, dn_a_log, dn_dt_bias, dn_norm_w, dn_w_out, swa_w_qkv, swa_b_qkv, swa_sinks,
           swa_w_o, swa_b_o, ffn_w_gu, ffn_w_down):
    Bp, Lp, D = x_prompt.shape
    Bs, Ls, _ = x_sample.shape
    depth = norm_mix.shape[0]
    n_delta = dn_w_in.shape[0]
    yp = x_prompt.reshape(Bp * Lp, D)
    ys = x_sample.reshape(Bs * Ls, D)
    R = cache_k.shape[2]

    w_in = jnp.pad(dn_w_in.astype(bf16), ((0, 0), (0, 0), (0, DN_IN_PAD - DN_IN_DIM)))
    w_zba = w_in[:, :, DN_CONV_DIM:]
    w_out = dn_w_out.astype(bf16)
    w_qkv = swa_w_qkv.astype(bf16)
    w_o = swa_w_o.astype(bf16)
    w_gu = ffn_w_gu.astype(bf16)
    w_down = ffn_w_down.astype(bf16)
    zero_b = jnp.zeros((D,), f32)
    zero_in = jnp.zeros((DN_IN_PAD,), f32)
    zero_zba = jnp.zeros((DN_ZBA_PAD,), f32)

    kp_list, vp_list, ks_list, vs_list = [], [], [], []
    prompt_state = None
    sample_states = None
    conv_p = []
    for i in range(depth):
        j = i // 2
        if i % 2 == 0:
            wts = (dn_conv_w[j], dn_a_log[j], dn_dt_bias[j], dn_norm_w[j])
            qkv, cb = proj_conv(yp, norm_mix[i], w_in, j, dn_conv_w[j], Lp)
            zba = rms_matmul(yp, norm_mix[i], w_zba, j, zero_zba, DN_ZBA_PAD)
            o, sn = delta_mixer((qkv.reshape(Bp, Lp, DN_CONV_DIM), zba.reshape(Bp, Lp, DN_ZBA_PAD)), None,
                                prompt_state, j, n_delta, *wts, C=min(DN_CHUNK, Lp), NS=1)
            prompt_state = (sn,)
            conv_p.append(cb)
            yp = matmul_resid(o.reshape(Bp * Lp, DN_VAL_DIM), w_out, j, zero_b, yp)
            proj = rms_matmul(ys, norm_mix[i], w_in, j, zero_in, 896).reshape(Bs, Ls, DN_IN_PAD)
            o, cb, sn = delta_mixer(proj, (state_conv, state_delta), sample_states, j, n_delta, *wts, C=Ls,
                                    NS=min(Bs, DN_SAMPLE_SEQS))
            sample_states = (cb, sn)
            ys = matmul_resid(o.reshape(Bs * Ls, DN_VAL_DIM), w_out, j, zero_b, ys)
        else:
            qkv = rms_matmul(yp, norm_mix[i], w_qkv, j, swa_b_qkv[j], 768).reshape(Bp, Lp, SWA_QKV_DIM)
            o = swa_prompt_attention(qkv, swa_sinks[j])
            yp = matmul_resid(o.reshape(Bp * Lp, SWA_Q_DIM), w_o, j, swa_b_o[j], yp)
            Rp = min(WINDOW, Lp)
            kp_list.append(qkv[:, Lp - Rp:, SWA_Q_DIM:SWA_Q_DIM + SWA_KV_DIM].reshape(Bp, Rp, SWA_KV_HEADS, SWA_HEAD_DIM))
            vp_list.append(qkv[:, Lp - Rp:, SWA_Q_DIM + SWA_KV_DIM:].reshape(Bp, Rp, SWA_KV_HEADS, SWA_HEAD_DIM))
            qkv = rms_matmul(ys, norm_mix[i], w_qkv, j, swa_b_qkv[j], 768).reshape(Bs, Ls, SWA_QKV_DIM)
            o = swa_sample_attention(qkv, cache_k[j].reshape(Bs, R, SWA_KV_DIM), cache_v[j].reshape(Bs, R, SWA_KV_DIM),
                                     swa_sinks[j])
            ys = matmul_resid(o.reshape(Bs * Ls, SWA_Q_DIM), w_o, j, swa_b_o[j], ys)
            ks_list.append(qkv[:, :, SWA_Q_DIM:SWA_Q_DIM + SWA_KV_DIM].reshape(Bs, Ls, SWA_KV_HEADS, SWA_HEAD_DIM))
            vs_list.append(qkv[:, :, SWA_Q_DIM + SWA_KV_DIM:].reshape(Bs, Ls, SWA_KV_HEADS, SWA_HEAD_DIM))
        last = i == depth - 1
        yp = ffn(yp, norm_ffn[i], w_gu, w_down, i, norm_final, last)
        ys = ffn(ys, norm_ffn[i], w_gu, w_down, i, norm_final, last)

    return (yp.reshape(Bp, Lp, D), ys.reshape(Bs, Ls, D),
            jnp.stack(conv_p), prompt_state[0], jnp.stack(kp_list), jnp.stack(vp_list),
            sample_states[0], sample_states[1], jnp.stack(ks_list), jnp.stack(vs_list))
```

```python
import functools

import jax
import jax.numpy as jnp
from jax import lax
from jax.experimental import pallas as pl
from jax.experimental.pallas import tpu as pltpu

f32 = jnp.float32
bf16 = jnp.bfloat16

D_MODEL = 1024
DN_QK_HEADS = 8
DN_V_HEADS = 16
DN_HEAD = 128
DN_KEY_DIM = DN_QK_HEADS * DN_HEAD
DN_VAL_DIM = DN_V_HEADS * DN_HEAD
DN_CONV_DIM = 2 * DN_KEY_DIM + DN_VAL_DIM
DN_CONV = 4
DN_IN_DIM = DN_CONV_DIM + DN_VAL_DIM + 2 * DN_V_HEADS
DN_IN_PAD = 6272
DN_ZBA_PAD = DN_IN_PAD - DN_CONV_DIM
SWA_HEADS = 16
SWA_KV_HEADS = 4
SWA_GROUPS = SWA_HEADS // SWA_KV_HEADS
SWA_HEAD_DIM = 64
SWA_Q_DIM = SWA_HEADS * SWA_HEAD_DIM
SWA_KV_DIM = SWA_KV_HEADS * SWA_HEAD_DIM
SWA_QKV_DIM = SWA_Q_DIM + 2 * SWA_KV_DIM
SWA_SCALE = SWA_HEAD_DIM ** -0.5
WINDOW = 128
D_FF = 2816
RMS_EPS = 1e-6
L2_EPS = 1e-6
ALIBI_SLOPES = tuple(float(2.0 ** (-8.0 * (i + 1) / SWA_HEADS)) for i in range(SWA_HEADS))

VMEM_LIMIT_BYTES = 56 * 1024 * 1024
DN_CHUNK = 64
SWA_SAMPLE_SEQS = 8
DN_SAMPLE_SEQS = 4
DN_PROMPT_SEQS = 2
SWA_HEAD_GROUP = 16


def _params(*sem):
    return pltpu.CompilerParams(dimension_semantics=sem, vmem_limit_bytes=VMEM_LIMIT_BYTES)


def _sigmoid(x):
    return 1.0 / (1.0 + jnp.exp(-x))


def _silu(x):
    h = 0.5 * x
    return h + h * jnp.tanh(h)


def _rms(x, w):
    return x * lax.rsqrt(jnp.mean(x * x, axis=-1, keepdims=True) + RMS_EPS) * w


def _dot(a, b):
    return jnp.dot(a, b, preferred_element_type=f32)


def _dot_nt(a, b):
    return lax.dot_general(a, b, (((1,), (1,)), ((), ())), preferred_element_type=f32)


def _dot_tn(a, b):
    return lax.dot_general(a, b, (((0,), (0,)), ((), ())), preferred_element_type=f32)


def _rms_matmul_kernel(x_ref, nw_ref, w_ref, b_ref, o_ref, xn_ref):
    @pl.when(pl.program_id(1) == 0)
    def _():
        xn_ref[...] = _rms(x_ref[...], nw_ref[...]).astype(bf16)

    o_ref[...] = _dot(xn_ref[...], w_ref[...]) + b_ref[...]


def rms_matmul(x, nw, w_stack, layer, b, tn):
    m, d = x.shape
    n = w_stack.shape[2]
    tm = min(m, 1024)
    return pl.pallas_call(
        _rms_matmul_kernel,
        grid=(m // tm, n // tn),
        in_specs=[
            pl.BlockSpec((tm, d), lambda i, j: (i, 0)),
            pl.BlockSpec((1, d), lambda i, j: (0, 0)),
            pl.BlockSpec((None, d, tn), lambda i, j: (layer, 0, j)),
            pl.BlockSpec((1, tn), lambda i, j: (0, j)),
        ],
        out_specs=pl.BlockSpec((tm, tn), lambda i, j: (i, j)),
        out_shape=jax.ShapeDtypeStruct((m, n), f32),
        scratch_shapes=[pltpu.VMEM((tm, d), bf16)],
        compiler_params=_params("parallel", "arbitrary"),
        name="rms_matmul",
    )(x, nw.reshape(1, d), w_stack, b.reshape(1, n))


def _proj_conv_kernel(tiles_per_seq, RS, x_ref, nw_ref, w_ref, cw_ref, o_ref, cs_ref, xn_ref, ext_ref, carry_ref):
    i = pl.program_id(0)
    j = pl.program_id(1)
    H = DN_HEAD
    tm, tn = o_ref.shape

    @pl.when(j == 0)
    def _():
        xn_ref[...] = _rms(x_ref[...], nw_ref[...]).astype(bf16)

    RC = 128

    def body(kind):
        ext_ref[0:8, :] = jnp.where(i % tiles_per_seq == 0, 0.0, carry_ref[j])
        cwb = [[jnp.broadcast_to(cw_ref[k:k + 1, hh * H:(hh + 1) * H], (RC, H)) for k in range(DN_CONV)]
               for hh in range(tn // H)]
        for r in range(tm // RS):
            ext_ref[8:8 + RS, :] = _dot(xn_ref[r * RS:(r + 1) * RS, :], w_ref[...])
            for c in range(RS // RC):
                for hh in range(tn // H):
                    cols = slice(hh * H, (hh + 1) * H)
                    e = ext_ref[c * RC:c * RC + RC + 8, cols]
                    y = e[8:] * cwb[hh][DN_CONV - 1]
                    for k in range(DN_CONV - 1):
                        y = y + pltpu.roll(e, DN_CONV - 1 - k, 0)[8:] * cwb[hh][k]
                    y = _silu(y)
                    if kind < 2:
                        inv = lax.rsqrt(jnp.sum(y * y, axis=-1, keepdims=True) + L2_EPS)
                        y = y * (inv * (H ** -0.5) if kind == 0 else inv)
                    o_ref[r * RS + c * RC:r * RS + (c + 1) * RC, cols] = y
            ext_ref[0:8, :] = ext_ref[RS:RS + 8, :]
        carry_ref[j] = ext_ref[0:8, :]
        cs_ref[0] = ext_ref[5:8, :]

    pl.when(j == 0)(functools.partial(body, 0))
    pl.when(j == 1)(functools.partial(body, 1))
    pl.when(j >= 2)(functools.partial(body, 2))


def proj_conv(x, nw, w_stack, layer, conv_w, seq_len):
    m, d = x.shape
    tm, tn, RS = 1024, DN_KEY_DIM, 512
    assert seq_len % tm == 0 and m % seq_len == 0
    tiles_per_seq = seq_len // tm
    nj = DN_CONV_DIM // tn
    qkv, tails = pl.pallas_call(
        functools.partial(_proj_conv_kernel, tiles_per_seq, RS),
        grid=(m // tm, nj),
        in_specs=[
            pl.BlockSpec((tm, d), lambda i, j: (i, 0)),
            pl.BlockSpec((1, d), lambda i, j: (0, 0)),
            pl.BlockSpec((None, d, tn), lambda i, j: (layer, 0, j)),
            pl.BlockSpec((DN_CONV, tn), lambda i, j: (0, j)),
        ],
        out_specs=[
            pl.BlockSpec((tm, tn), lambda i, j: (i, j)),
            pl.BlockSpec((1, DN_CONV - 1, tn), lambda i, j: (i, 0, j)),
        ],
        out_shape=[
            jax.ShapeDtypeStruct((m, DN_CONV_DIM), f32),
            jax.ShapeDtypeStruct((m // tm, DN_CONV - 1, DN_CONV_DIM), f32),
        ],
        scratch_shapes=[
            pltpu.VMEM((tm, d), bf16),
            pltpu.VMEM((RS + 8, tn), f32),
            pltpu.VMEM((nj, 8, tn), f32),
        ],
        compiler_params=_params("arbitrary", "arbitrary"),
        name="proj_conv",
    )(x, nw.reshape(1, d), w_stack, conv_w)
    return qkv, tails.reshape(m // seq_len, tiles_per_seq, DN_CONV - 1, DN_CONV_DIM)[:, -1]


def _matmul_resid_kernel(a_ref, w_ref, b_ref, r_ref, o_ref):
    o_ref[...] = r_ref[...] + _dot(a_ref[...], w_ref[...]) + b_ref[...]


def matmul_resid(a, w_stack, layer, b, resid):
    m, k = a.shape
    n = w_stack.shape[2]
    tm = min(m, 1024)
    return pl.pallas_call(
        _matmul_resid_kernel,
        grid=(m // tm,),
        in_specs=[
            pl.BlockSpec((tm, k), lambda i: (i, 0)),
            pl.BlockSpec((None, k, n), lambda i: (layer, 0, 0)),
            pl.BlockSpec((1, n), lambda i: (0, 0)),
            pl.BlockSpec((tm, n), lambda i: (i, 0)),
        ],
        out_specs=pl.BlockSpec((tm, n), lambda i: (i, 0)),
        out_shape=jax.ShapeDtypeStruct((m, n), f32),
        compiler_params=_params("parallel"),
        name="matmul_resid",
    )(a, w_stack, b.reshape(1, n), resid)


def _ffn_kernel(final_norm, x_ref, nw_ref, wg_ref, wu_ref, wd_ref, nf_ref, o_ref, xn_ref, acc_ref):
    j = pl.program_id(1)

    @pl.when(j == 0)
    def _():
        xn_ref[...] = _rms(x_ref[...], nw_ref[...]).astype(bf16)
        acc_ref[...] = jnp.zeros_like(acc_ref)

    xn = xn_ref[...]
    tf = wg_ref.shape[1]
    half = (tf // 256) * 128
    for c0, c1 in ((0, half), (half, tf)):
        g = _dot(xn, wg_ref[:, c0:c1])
        u = _dot(xn, wu_ref[:, c0:c1])
        h = (_silu(g) * u).astype(bf16)
        acc_ref[...] += _dot(h, wd_ref[c0:c1, :])

    @pl.when(j == pl.num_programs(1) - 1)
    def _():
        y = x_ref[...] + acc_ref[...]
        if final_norm:
            y = _rms(y, nf_ref[...])
        o_ref[...] = y


def ffn(x, nw, wgu_stack, wd_stack, layer, nf, final_norm):
    m, d = x.shape
    tm = min(m, 1024)
    tf = D_FF // 2
    nj = D_FF // tf
    return pl.pallas_call(
        functools.partial(_ffn_kernel, final_norm),
        grid=(m // tm, nj),
        in_specs=[
            pl.BlockSpec((tm, d), lambda i, j: (i, 0)),
            pl.BlockSpec((1, d), lambda i, j: (0, 0)),
            pl.BlockSpec((None, d, tf), lambda i, j: (layer, 0, j)),
            pl.BlockSpec((None, d, tf), lambda i, j: (layer, 0, j + nj)),
            pl.BlockSpec((None, tf, d), lambda i, j: (layer, j, 0)),
            pl.BlockSpec((1, d), lambda i, j: (0, 0)),
        ],
        out_specs=pl.BlockSpec((tm, d), lambda i, j: (i, 0)),
        out_shape=jax.ShapeDtypeStruct((m, d), f32),
        scratch_shapes=[pltpu.VMEM((tm, d), bf16), pltpu.VMEM((tm, d), f32)],
        compiler_params=_params("parallel", "arbitrary"),
        name="ffn",
    )(x, nw.reshape(1, d), wgu_stack, wgu_stack, wd_stack, nf.reshape(1, d))


def _split3(x):
    hi = x.astype(bf16)
    r1 = x - hi.astype(f32)
    mid = r1.astype(bf16)
    lo = (r1 - mid.astype(f32)).astype(bf16)
    return hi, mid, lo


def _delta_kernel(C, NS, n_iter, has_init, n_alias, pre_conv, *refs):
    if pre_conv:
        qkv_ref, zba_ref = refs[0:2]
        pos = 2
    else:
        proj_ref = refs[0]
        pos = 1
    if has_init:
        conv0_ref, s0_ref = refs[pos:pos + 2]
        pos += 2
    cw_ref, alog_ref, dtb_ref, nw_ref = refs[pos:pos + 4]
    pos += 4 + n_alias
    if pre_conv:
        o_ref, sout_ref = refs[pos:pos + 2]
    else:
        o_ref, convout_ref, sout_ref = refs[pos:pos + 3]
        full_ref, qn_ref, kn_ref = refs[pos + 3:]
    n = pl.program_id(1)
    H = DN_HEAD
    NH = DN_V_HEADS
    NQ = DN_QK_HEADS
    per = NH // NQ
    seqs = range(NS)
    pairs = [(s, h) for s in seqs for h in range(NH)]
    qkp = [(s, qh) for s in seqs for qh in range(NQ)]
    P = range(len(pairs))
    qi = [s * NQ + h // per for s, h in pairs]

    @pl.when(n == 0)
    def _():
        if has_init:
            full_ref[:, 5:8, :] = conv0_ref[...]
            sout_ref[...] = s0_ref[...]
        else:
            if not pre_conv:
                full_ref[:, 5:8, :] = jnp.zeros((NS, DN_CONV - 1, DN_CONV_DIM), f32)
            sout_ref[...] = jnp.zeros_like(sout_ref)

    if pre_conv:
        get_ba = lambda s: zba_ref[s, :, DN_VAL_DIM:DN_VAL_DIM + H]
        get_z = lambda s, h: zba_ref[s, :, h * H:(h + 1) * H]
    else:
        get_ba = lambda s: proj_ref[s, :, DN_CONV_DIM + DN_VAL_DIM:DN_IN_PAD]
        get_z = lambda s, h: proj_ref[s, :, DN_CONV_DIM + h * H:DN_CONV_DIM + (h + 1) * H]
        for s in seqs:
            full_ref[s, 8:8 + C, :] = proj_ref[s, :, 0:DN_CONV_DIM]

    def conv_silu(s, cb):
        cols = slice(cb * H, (cb + 1) * H)
        w = cw_ref[:, cols]
        y = (full_ref[s, 5:5 + C, cols] * w[0:1] + full_ref[s, 6:6 + C, cols] * w[1:2]
             + full_ref[s, 7:7 + C, cols] * w[2:3] + full_ref[s, 8:8 + C, cols] * w[3:4])
        return _silu(y)

    if not pre_conv:
        for s in seqs:
            for cb in range(2 * NQ):
                y = conv_silu(s, cb)
                inv = lax.rsqrt(jnp.sum(y * y, axis=-1, keepdims=True) + L2_EPS)
                if cb < NQ:
                    qn_ref[s * NQ + cb] = y * (inv * (H ** -0.5))
                else:
                    kn_ref[s * NQ + cb - NQ] = y * inv

    ri = lax.broadcasted_iota(jnp.int32, (C, C), 0)
    ci = lax.broadcasted_iota(jnp.int32, (C, C), 1)
    causal = ri >= ci
    strict = ri > ci
    eye = jnp.where(ri == ci, 1.0, 0.0).astype(f32)
    tri16 = jnp.where(causal, 1.0, 0.0).astype(bf16)
    beta, G, GT = [], [], []
    for s in seqs:
        ba = get_ba(s)
        beta.append(_sigmoid(ba[:, 0:NH]))
        xa = ba[:, NH:2 * NH] + dtb_ref[...]
        softplus = jnp.maximum(xa, 0.0) + jnp.log(1.0 + jnp.exp(-jnp.abs(xa)))
        g = -jnp.exp(alog_ref[...]) * softplus
        g3 = jnp.concatenate(_split3(g), axis=1)
        G3 = _dot(tri16, g3)
        G.append(G3[:, 0:NH] + G3[:, NH:2 * NH] + G3[:, 2 * NH:3 * NH])
        GT3 = lax.dot_general(g3, tri16, (((0,), (1,)), ((), ())), preferred_element_type=f32)
        GT.append(GT3[0:NH] + GT3[NH:2 * NH] + GT3[2 * NH:3 * NH])

    if pre_conv:
        qs = [qkv_ref[s, :, qh * H:(qh + 1) * H] for s, qh in qkp]
        ks = [qkv_ref[s, :, DN_KEY_DIM + qh * H:DN_KEY_DIM + (qh + 1) * H] for s, qh in qkp]
    else:
        ks = [kn_ref[i] for i in range(len(qkp))]
        qs = [qn_ref[i] for i in range(len(qkp))]
    kqs = [_dot_nt(jnp.concatenate([ks[i], qs[i]], axis=0).astype(bf16), ks[i].astype(bf16))
           for i in range(len(qkp))]
    kTs = [ks[i].T for i in range(len(qkp))]
    Gc = [G[s][:, h:h + 1] for s, h in pairs]
    Gr = [GT[s][h:h + 1, :] for s, h in pairs]
    bc = [beta[s][:, h:h + 1] for s, h in pairs]
    decay = [jnp.where(causal, jnp.exp(jnp.where(causal, Gc[p] - Gr[p], 0.0)), 0.0) for p in P]
    M = [jnp.where(strict, -(bc[p] * kqs[qi[p]][:C] * decay[p]), 0.0) for p in P]
    T = [eye + M[p] for p in P]
    Mp = [_dot(M[p].astype(bf16), M[p].astype(bf16)) for p in P]
    for it in range(n_iter):
        if it < n_iter - 1:
            tm = [_dot(jnp.concatenate([T[p], Mp[p]], axis=0).astype(bf16), Mp[p].astype(bf16)) for p in P]
            T = [T[p] + tm[p][:C] for p in P]
            Mp = [tm[p][C:] for p in P]
        else:
            tm = [_dot(T[p].astype(bf16), Mp[p].astype(bf16)) for p in P]
            T = [T[p] + tm[p] for p in P]
    if pre_conv:
        vs = [qkv_ref[s, :, 2 * DN_KEY_DIM + h * H:2 * DN_KEY_DIM + (h + 1) * H] for s, h in pairs]
    else:
        vs = [conv_silu(s, 2 * NQ + h) for s, h in pairs]
        for s in seqs:
            tail = full_ref[s, C + 5:C + 8, :]
            full_ref[s, 5:8, :] = tail
            convout_ref[s] = tail

    eG = [jnp.exp(Gc[p]) for p in P]
    uw = [_dot(T[p].astype(bf16),
               jnp.concatenate([vs[p] * bc[p], ks[qi[p]] * (bc[p] * eG[p])], axis=1).astype(bf16))
          for p in P]
    S = [sout_ref[s, h] for s, h in pairs]
    ws = [_dot(jnp.concatenate([uw[p][:, H:], qs[qi[p]] * eG[p]], axis=0).astype(bf16), S[p].astype(bf16))
          for p in P]
    unew = [uw[p][:, :H] - ws[p][:C] for p in P]
    Gl = [Gc[p][C - 1:C, :] for p in P]
    r = [_dot(jnp.concatenate([kqs[qi[p]][C:] * decay[p], kTs[qi[p]] * jnp.exp(Gl[p] - Gr[p])],
                              axis=0).astype(bf16), unew[p].astype(bf16))
         for p in P]
    for p, (s, h) in enumerate(pairs):
        sout_ref[s, h] = S[p] * jnp.exp(Gl[p]) + r[p][C:]
        on = _rms(ws[p][C:] + r[p][:C], nw_ref[...])
        o_ref[s, :, h * H:(h + 1) * H] = (on * _silu(get_z(s, h))).astype(bf16)


def delta_mixer(proj, init, prev_out, layer, n_layers, conv_w, a_log, dt_bias, norm_w, C, NS):
    pre_conv = isinstance(proj, tuple)
    assert not (pre_conv and init is not None)
    B, L, _ = proj[0].shape if pre_conv else proj.shape
    n_iter = max(1, (C - 1).bit_length()) - 1
    H = DN_HEAD
    has_init = init is not None
    n_alias = 0 if prev_out is None else len(prev_out)
    conv_spec = pl.BlockSpec((None, NS, DN_CONV - 1, DN_CONV_DIM), lambda b, n: (layer, b, 0, 0))
    state_spec = pl.BlockSpec((None, NS, DN_V_HEADS, H, H), lambda b, n: (layer, b, 0, 0, 0))
    small = lambda shape: pl.BlockSpec(shape, lambda b, n: (0,) * len(shape))
    if pre_conv:
        in_specs = [pl.BlockSpec((NS, C, DN_CONV_DIM), lambda b, n: (b, n, 0)),
                    pl.BlockSpec((NS, C, DN_ZBA_PAD), lambda b, n: (b, n, 0))]
        args = list(proj)
    else:
        in_specs = [pl.BlockSpec((NS, C, DN_IN_PAD), lambda b, n: (b, n, 0))]
        args = [proj]
    if has_init:
        in_specs += [conv_spec, state_spec]
        args += list(init)
    in_specs += [small((DN_CONV, DN_CONV_DIM)), small((1, DN_V_HEADS)), small((1, DN_V_HEADS)), small((1, H))]
    args += [conv_w, a_log.reshape(1, -1), dt_bias.reshape(1, -1), norm_w.reshape(1, -1)]
    out_specs = [pl.BlockSpec((NS, C, DN_VAL_DIM), lambda b, n: (b, n, 0)), conv_spec, state_spec]
    out_shape = [
        jax.ShapeDtypeStruct((B, L, DN_VAL_DIM), bf16),
        jax.ShapeDtypeStruct((n_layers, B, DN_CONV - 1, DN_CONV_DIM), f32),
        jax.ShapeDtypeStruct((n_layers, B, DN_V_HEADS, H, H), f32),
    ]
    scratch_shapes = [
        pltpu.VMEM((NS, C + 8, DN_CONV_DIM), f32),
        pltpu.VMEM((NS * DN_QK_HEADS, C, H), f32),
        pltpu.VMEM((NS * DN_QK_HEADS, C, H), f32),
    ]
    if pre_conv:
        del out_specs[1], out_shape[1]
        scratch_shapes = []
    aliases = {}
    if prev_out is not None:
        first_out = len(out_specs) - len(prev_out)
        aliases = {len(args) + k: first_out + k for k in range(len(prev_out))}
        in_specs += [pl.BlockSpec(memory_space=pl.ANY)] * len(prev_out)
        args += list(prev_out)
    return pl.pallas_call(
        functools.partial(_delta_kernel, C, NS, n_iter, has_init, n_alias, pre_conv),
        grid=(B // NS, L // C),
        in_specs=in_specs,
        out_specs=out_specs,
        out_shape=out_shape,
        scratch_shapes=scratch_shapes,
        input_output_aliases=aliases,
        compiler_params=_params("parallel", "arbitrary"),
        name="delta_mixer",
    )(*args)


def _swa_bias_table():
    W = WINDOW
    ri = lax.broadcasted_iota(jnp.int32, (W, 2 * W), 0)
    ci = lax.broadcasted_iota(jnp.int32, (W, 2 * W), 1)
    rel = W + ri - ci
    band = (rel >= 0) & (rel <= WINDOW)
    slopes = jnp.asarray(ALIBI_SLOPES, f32).reshape(SWA_HEADS, 1, 1)
    alibi = -slopes * rel.astype(f32)[None]
    later = jnp.where(band[None], alibi, -jnp.inf)
    first = jnp.where((band & (ci >= W))[None], alibi, -jnp.inf)
    return jnp.stack([first, later])


def _swa_prompt_kernel(NB, sinks_ref, bias_ref, q_ref, kc_ref, kp_ref, vc_ref, vp_ref, o_ref):
    hd = SWA_HEAD_DIM
    k16, v16 = [], []
    for b in range(NB):
        for kv in range(SWA_KV_HEADS):
            cs = slice(kv * hd, (kv + 1) * hd)
            k16.append(jnp.concatenate([kp_ref[b, :, cs], kc_ref[b, :, cs]], axis=0).astype(bf16))
            v16.append(jnp.concatenate([vp_ref[b, :, cs], vc_ref[b, :, cs]], axis=0).astype(bf16))
    all_pairs = [(b, h) for b in range(NB) for h in range(SWA_HEADS)]
    for g0 in range(0, len(all_pairs), SWA_HEAD_GROUP):
        pairs = all_pairs[g0:g0 + SWA_HEAD_GROUP]
        kvi = [b * SWA_KV_HEADS + h // SWA_GROUPS for b, h in pairs]
        s = [_dot_nt((q_ref[b, :, h * hd:(h + 1) * hd] * SWA_SCALE).astype(bf16), k16[kvi[i]])
             for i, (b, h) in enumerate(pairs)]
        p = []
        for i, (b, h) in enumerate(pairs):
            sh = s[i] + bias_ref[0, h]
            sk = sinks_ref[h]
            m = jnp.maximum(jnp.max(sh, axis=-1, keepdims=True), sk)
            e = jnp.exp(sh - m)
            denom = jnp.sum(e, axis=-1, keepdims=True) + jnp.exp(sk - m)
            p.append((e / denom).astype(bf16))
        o = [_dot(p[i], v16[kvi[i]]) for i in range(len(pairs))]
        for i, (b, h) in enumerate(pairs):
            o_ref[b, :, h * hd:(h + 1) * hd] = o[i].astype(bf16)


def swa_prompt_attention(qkv, sinks):
    B, L, _ = qkv.shape
    W = WINDOW
    kb = SWA_Q_DIM // SWA_KV_DIM
    prev = lambda n: jnp.maximum(n - 1, 0)
    NB = 1
    return pl.pallas_call(
        functools.partial(_swa_prompt_kernel, NB),
        grid=(B // NB, L // W),
        in_specs=[
            pl.BlockSpec(memory_space=pltpu.SMEM),
            pl.BlockSpec((1, SWA_HEADS, W, 2 * W), lambda b, n: (jnp.minimum(n, 1), 0, 0, 0)),
            pl.BlockSpec((NB, W, SWA_Q_DIM), lambda b, n: (b, n, 0)),
            pl.BlockSpec((NB, W, SWA_KV_DIM), lambda b, n: (b, n, kb)),
            pl.BlockSpec((NB, W, SWA_KV_DIM), lambda b, n: (b, prev(n), kb)),
            pl.BlockSpec((NB, W, SWA_KV_DIM), lambda b, n: (b, n, kb + 1)),
            pl.BlockSpec((NB, W, SWA_KV_DIM), lambda b, n: (b, prev(n), kb + 1)),
        ],
        out_specs=pl.BlockSpec((NB, W, SWA_Q_DIM), lambda b, n: (b, n, 0)),
        out_shape=jax.ShapeDtypeStruct((B, L, SWA_Q_DIM), bf16),
        compiler_params=_params("parallel", "arbitrary"),
        name="swa_prompt",
    )(sinks, _swa_bias_table(), qkv, qkv, qkv, qkv, qkv)


def _swa_sample_kernel(NS, L, R, sinks_ref, qkv_ref, ck_ref, cv_ref, o_ref):
    hd = SWA_HEAD_DIM
    G = SWA_GROUPS
    rows = G * L
    t_c = lax.broadcasted_iota(jnp.int32, (rows, R), 0) % L
    j_c = lax.broadcasted_iota(jnp.int32, (rows, R), 1)
    rel_c = R + t_c - j_c
    valid_c = (rel_c >= 0) & (rel_c <= WINDOW)
    t_n = lax.broadcasted_iota(jnp.int32, (rows, L), 0) % L
    j_n = lax.broadcasted_iota(jnp.int32, (rows, L), 1)
    rel_n = t_n - j_n
    valid_n = (rel_n >= 0) & (rel_n <= WINDOW)
    g_row = lax.broadcasted_iota(jnp.int32, (rows, 1), 0) // L
    bias_c, bias_n, sks = [], [], []
    for kv in range(SWA_KV_HEADS):
        slope = jnp.zeros((rows, 1), f32)
        sk = jnp.zeros((rows, 1), f32)
        for gi in range(G):
            h = kv * G + gi
            slope = jnp.where(g_row == gi, ALIBI_SLOPES[h], slope)
            sk = jnp.where(g_row == gi, sinks_ref[h], sk)
        bias_c.append(slope * rel_c.astype(f32))
        bias_n.append(slope * rel_n.astype(f32))
        sks.append(sk)
    pairs = [(b, kv) for b in range(NS) for kv in range(SWA_KV_HEADS)]
    kcol = lambda kv: slice(SWA_Q_DIM + kv * hd, SWA_Q_DIM + (kv + 1) * hd)
    vcol = lambda kv: slice(SWA_Q_DIM + SWA_KV_DIM + kv * hd, SWA_Q_DIM + SWA_KV_DIM + (kv + 1) * hd)
    q16 = [jnp.concatenate([qkv_ref[b, :, (kv * G + gi) * hd:(kv * G + gi + 1) * hd] for gi in range(G)],
                           axis=0).astype(bf16) for b, kv in pairs]
    s_c = [_dot_nt(q16[i], ck_ref[b, :, kv * hd:(kv + 1) * hd].astype(bf16)) for i, (b, kv) in enumerate(pairs)]
    s_n = [_dot_nt(q16[i], qkv_ref[b, :, kcol(kv)].astype(bf16)) for i, (b, kv) in enumerate(pairs)]
    p_c, p_n = [], []
    for i, (b, kv) in enumerate(pairs):
        sc = jnp.where(valid_c, s_c[i] * SWA_SCALE - bias_c[kv], -jnp.inf)
        sn = jnp.where(valid_n, s_n[i] * SWA_SCALE - bias_n[kv], -jnp.inf)
        m = jnp.maximum(jnp.maximum(jnp.max(sc, axis=-1, keepdims=True),
                                    jnp.max(sn, axis=-1, keepdims=True)), sks[kv])
        ec = jnp.exp(sc - m)
        en = jnp.exp(sn - m)
        denom = jnp.sum(ec, axis=-1, keepdims=True) + jnp.sum(en, axis=-1, keepdims=True) + jnp.exp(sks[kv] - m)
        p_c.append((ec / denom).astype(bf16))
        p_n.append((en / denom).astype(bf16))
    o = [_dot(p_c[i], cv_ref[b, :, kv * hd:(kv + 1) * hd].astype(bf16))
         + _dot(p_n[i], qkv_ref[b, :, vcol(kv)].astype(bf16)) for i, (b, kv) in enumerate(pairs)]
    for i, (b, kv) in enumerate(pairs):
        for gi in range(G):
            h = kv * G + gi
            o_ref[b, :, h * hd:(h + 1) * hd] = o[i][gi * L:(gi + 1) * L].astype(bf16)


def swa_sample_attention(qkv, cache_k, cache_v, sinks):
    B, L, _ = qkv.shape
    R = cache_k.shape[1]
    NS = min(B, SWA_SAMPLE_SEQS)
    return pl.pallas_call(
        functools.partial(_swa_sample_kernel, NS, L, R),
        grid=(B // NS,),
        in_specs=[
            pl.BlockSpec(memory_space=pltpu.SMEM),
            pl.BlockSpec((NS, L, SWA_QKV_DIM), lambda b: (b, 0, 0)),
            pl.BlockSpec((NS, R, SWA_KV_DIM), lambda b: (b, 0, 0)),
            pl.BlockSpec((NS, R, SWA_KV_DIM), lambda b: (b, 0, 0)),
        ],
        out_specs=pl.BlockSpec((NS, L, SWA_Q_DIM), lambda b: (b, 0, 0)),
        out_shape=jax.ShapeDtypeStruct((B, L, SWA_Q_DIM), bf16),
        compiler_params=_params("parallel"),
        name="swa_sample",
    )(sinks, qkv, cache_k, cache_v)


def kernel(x_prompt, x_sample, state_conv, state_delta, cache_k, cache_v, norm_mix, norm_ffn, norm_final,
           dn_w_in, dn_conv_w, dn_a_log, dn_dt_bias, dn_norm_w, dn_w_out, swa_w_qkv, swa_b_qkv, swa_sinks,
           swa_w_o, swa_b_o, ffn_w_gu, ffn_w_down):
    Bp, Lp, D = x_prompt.shape
    Bs, Ls, _ = x_sample.shape
    depth = norm_mix.shape[0]
    n_delta = dn_w_in.shape[0]
    yp = x_prompt.reshape(Bp * Lp, D)
    ys = x_sample.reshape(Bs * Ls, D)
    R = cache_k.shape[2]

    w_in = jnp.pad(dn_w_in.astype(bf16), ((0, 0), (0, 0), (0, DN_IN_PAD - DN_IN_DIM)))
    w_zba = w_in[:, :, DN_CONV_DIM:]
    w_out = dn_w_out.astype(bf16)
    w_qkv = swa_w_qkv.astype(bf16)
    w_o = swa_w_o.astype(bf16)
    w_gu = ffn_w_gu.astype(bf16)
    w_down = ffn_w_down.astype(bf16)
    zero_b = jnp.zeros((D,), f32)
    zero_in = jnp.zeros((DN_IN_PAD,), f32)
    zero_zba = jnp.zeros((DN_ZBA_PAD,), f32)

    kp_list, vp_list, ks_list, vs_list = [], [], [], []
    prompt_state = None
    sample_states = None
    conv_p = []
    for i in range(depth):
        j = i // 2
        if i % 2 == 0:
            wts = (dn_conv_w[j], dn_a_log[j], dn_dt_bias[j], dn_norm_w[j])
            qkv, cb = proj_conv(yp, norm_mix[i], w_in, j, dn_conv_w[j], Lp)
            zba = rms_matmul(yp, norm_mix[i], w_zba, j, zero_zba, DN_ZBA_PAD)
            o, sn = delta_mixer((qkv.reshape(Bp, Lp, DN_CONV_DIM), zba.reshape(Bp, Lp, DN_ZBA_PAD)), None,
                                prompt_state, j, n_delta, *wts, C=min(DN_CHUNK, Lp), NS=DN_PROMPT_SEQS)
            prompt_state = (sn,)
            conv_p.append(cb)
            yp = matmul_resid(o.reshape(Bp * Lp, DN_VAL_DIM), w_out, j, zero_b, yp)
            proj = rms_matmul(ys, norm_mix[i], w_in, j, zero_in, 896).reshape(Bs, Ls, DN_IN_PAD)
            o, cb, sn = delta_mixer(proj, (state_conv, state_delta), sample_states, j, n_delta, *wts, C=Ls,
                                    NS=min(Bs, DN_SAMPLE_SEQS))
            sample_states = (cb, sn)
            ys = matmul_resid(o.reshape(Bs * Ls, DN_VAL_DIM), w_out, j, zero_b, ys)
        else:
            qkv = rms_matmul(yp, norm_mix[i], w_qkv, j, swa_b_qkv[j], 768).reshape(Bp, Lp, SWA_QKV_DIM)
            o = swa_prompt_attention(qkv, swa_sinks[j])
            yp = matmul_resid(o.reshape(Bp * Lp, SWA_Q_DIM), w_o, j, swa_b_o[j], yp)
            Rp = min(WINDOW, Lp)
            kp_list.append(qkv[:, Lp - Rp:, SWA_Q_DIM:SWA_Q_DIM + SWA_KV_DIM].reshape(Bp, Rp, SWA_KV_HEADS, SWA_HEAD_DIM))
            vp_list.append(qkv[:, Lp - Rp:, SWA_Q_DIM + SWA_KV_DIM:].reshape(Bp, Rp, SWA_KV_HEADS, SWA_HEAD_DIM))
            qkv = rms_matmul(ys, norm_mix[i], w_qkv, j, swa_b_qkv[j], 768).reshape(Bs, Ls, SWA_QKV_DIM)
            o = swa_sample_attention(qkv, cache_k[j].reshape(Bs, R, SWA_KV_DIM), cache_v[j].reshape(Bs, R, SWA_KV_DIM),
                                     swa_sinks[j])
            ys = matmul_resid(o.reshape(Bs * Ls, SWA_Q_DIM), w_o, j, swa_b_o[j], ys)
            ks_list.append(qkv[:, :, SWA_Q_DIM:SWA_Q_DIM + SWA_KV_DIM].reshape(Bs, Ls, SWA_KV_HEADS, SWA_HEAD_DIM))
            vs_list.append(qkv[:, :, SWA_Q_DIM + SWA_KV_DIM:].reshape(Bs, Ls, SWA_KV_HEADS, SWA_HEAD_DIM))
        last = i == depth - 1
        yp = ffn(yp, norm_ffn[i], w_gu, w_down, i, norm_final, last)
        ys = ffn(ys, norm_ffn[i], w_gu, w_down, i, norm_final, last)

    return (yp.reshape(Bp, Lp, D), ys.reshape(Bs, Ls, D),
            jnp.stack(conv_p), prompt_state[0], jnp.stack(kp_list), jnp.stack(vp_list),
            sample_states[0], sample_states[1], jnp.stack(ks_list), jnp.stack(vs_list))
```

```python
import functools

import jax
import jax.numpy as jnp
from jax import lax
from jax.experimental import pallas as pl
from jax.experimental.pallas import tpu as pltpu

f32 = jnp.float32
bf16 = jnp.bfloat16

D_MODEL = 1024
DN_QK_HEADS = 8
DN_V_HEADS = 16
DN_HEAD = 128
DN_KEY_DIM = DN_QK_HEADS * DN_HEAD
DN_VAL_DIM = DN_V_HEADS * DN_HEAD
DN_CONV_DIM = 2 * DN_KEY_DIM + DN_VAL_DIM
DN_CONV = 4
DN_IN_DIM = DN_CONV_DIM + DN_VAL_DIM + 2 * DN_V_HEADS
DN_IN_PAD = 6272
DN_ZBA_PAD = DN_IN_PAD - DN_CONV_DIM
SWA_HEADS = 16
SWA_KV_HEADS = 4
SWA_GROUPS = SWA_HEADS // SWA_KV_HEADS
SWA_HEAD_DIM = 64
SWA_Q_DIM = SWA_HEADS * SWA_HEAD_DIM
SWA_KV_DIM = SWA_KV_HEADS * SWA_HEAD_DIM
SWA_QKV_DIM = SWA_Q_DIM + 2 * SWA_KV_DIM
SWA_SCALE = SWA_HEAD_DIM ** -0.5
WINDOW = 128
D_FF = 2816
RMS_EPS = 1e-6
L2_EPS = 1e-6
ALIBI_SLOPES = tuple(float(2.0 ** (-8.0 * (i + 1) / SWA_HEADS)) for i in range(SWA_HEADS))

VMEM_LIMIT_BYTES = 56 * 1024 * 1024
DN_CHUNK = 64
SWA_SAMPLE_SEQS = 8
DN_SAMPLE_SEQS = 4
DN_PROMPT_SEQS = 4
SWA_HEAD_GROUP = 16


def _params(*sem):
    return pltpu.CompilerParams(dimension_semantics=sem, vmem_limit_bytes=VMEM_LIMIT_BYTES)


def _sigmoid(x):
    return 1.0 / (1.0 + jnp.exp(-x))


def _silu(x):
    h = 0.5 * x
    return h + h * jnp.tanh(h)


def _rms(x, w):
    return x * lax.rsqrt(jnp.mean(x * x, axis=-1, keepdims=True) + RMS_EPS) * w


def _dot(a, b):
    return jnp.dot(a, b, preferred_element_type=f32)


def _dot_nt(a, b):
    return lax.dot_general(a, b, (((1,), (1,)), ((), ())), preferred_element_type=f32)


def _dot_tn(a, b):
    return lax.dot_general(a, b, (((0,), (0,)), ((), ())), preferred_element_type=f32)


def _rms_matmul_kernel(x_ref, nw_ref, w_ref, b_ref, o_ref, xn_ref):
    @pl.when(pl.program_id(1) == 0)
    def _():
        xn_ref[...] = _rms(x_ref[...], nw_ref[...]).astype(bf16)

    o_ref[...] = _dot(xn_ref[...], w_ref[...]) + b_ref[...]


def rms_matmul(x, nw, w_stack, layer, b, tn):
    m, d = x.shape
    n = w_stack.shape[2]
    tm = min(m, 1024)
    return pl.pallas_call(
        _rms_matmul_kernel,
        grid=(m // tm, n // tn),
        in_specs=[
            pl.BlockSpec((tm, d), lambda i, j: (i, 0)),
            pl.BlockSpec((1, d), lambda i, j: (0, 0)),
            pl.BlockSpec((None, d, tn), lambda i, j: (layer, 0, j)),
            pl.BlockSpec((1, tn), lambda i, j: (0, j)),
        ],
        out_specs=pl.BlockSpec((tm, tn), lambda i, j: (i, j)),
        out_shape=jax.ShapeDtypeStruct((m, n), f32),
        scratch_shapes=[pltpu.VMEM((tm, d), bf16)],
        compiler_params=_params("parallel", "arbitrary"),
        name="rms_matmul",
    )(x, nw.reshape(1, d), w_stack, b.reshape(1, n))


def _proj_conv_kernel(tiles_per_seq, RS, x_ref, nw_ref, w_ref, cw_ref, o_ref, cs_ref, xn_ref, ext_ref, carry_ref):
    i = pl.program_id(0)
    j = pl.program_id(1)
    H = DN_HEAD
    tm, tn = o_ref.shape

    @pl.when(j == 0)
    def _():
        xn_ref[...] = _rms(x_ref[...], nw_ref[...]).astype(bf16)

    RC = 128

    def body(kind):
        ext_ref[0:8, :] = jnp.where(i % tiles_per_seq == 0, 0.0, carry_ref[j])
        cwb = [[jnp.broadcast_to(cw_ref[k:k + 1, hh * H:(hh + 1) * H], (RC, H)) for k in range(DN_CONV)]
               for hh in range(tn // H)]
        for r in range(tm // RS):
            ext_ref[8:8 + RS, :] = _dot(xn_ref[r * RS:(r + 1) * RS, :], w_ref[...])
            for c in range(RS // RC):
                for hh in range(tn // H):
                    cols = slice(hh * H, (hh + 1) * H)
                    e = ext_ref[c * RC:c * RC + RC + 8, cols]
                    y = e[8:] * cwb[hh][DN_CONV - 1]
                    for k in range(DN_CONV - 1):
                        y = y + pltpu.roll(e, DN_CONV - 1 - k, 0)[8:] * cwb[hh][k]
                    y = _silu(y)
                    if kind < 2:
                        inv = lax.rsqrt(jnp.sum(y * y, axis=-1, keepdims=True) + L2_EPS)
                        y = y * (inv * (H ** -0.5) if kind == 0 else inv)
                    o_ref[r * RS + c * RC:r * RS + (c + 1) * RC, cols] = y
            ext_ref[0:8, :] = ext_ref[RS:RS + 8, :]
        carry_ref[j] = ext_ref[0:8, :]
        cs_ref[0] = ext_ref[5:8, :]

    pl.when(j == 0)(functools.partial(body, 0))
    pl.when(j == 1)(functools.partial(body, 1))
    pl.when(j >= 2)(functools.partial(body, 2))


def proj_conv(x, nw, w_stack, layer, conv_w, seq_len):
    m, d = x.shape
    tm, tn, RS = 1024, DN_KEY_DIM, 512
    assert seq_len % tm == 0 and m % seq_len == 0
    tiles_per_seq = seq_len // tm
    nj = DN_CONV_DIM // tn
    qkv, tails = pl.pallas_call(
        functools.partial(_proj_conv_kernel, tiles_per_seq, RS),
        grid=(m // tm, nj),
        in_specs=[
            pl.BlockSpec((tm, d), lambda i, j: (i, 0)),
            pl.BlockSpec((1, d), lambda i, j: (0, 0)),
            pl.BlockSpec((None, d, tn), lambda i, j: (layer, 0, j)),
            pl.BlockSpec((DN_CONV, tn), lambda i, j: (0, j)),
        ],
        out_specs=[
            pl.BlockSpec((tm, tn), lambda i, j: (i, j)),
            pl.BlockSpec((1, DN_CONV - 1, tn), lambda i, j: (i, 0, j)),
        ],
        out_shape=[
            jax.ShapeDtypeStruct((m, DN_CONV_DIM), f32),
            jax.ShapeDtypeStruct((m // tm, DN_CONV - 1, DN_CONV_DIM), f32),
        ],
        scratch_shapes=[
            pltpu.VMEM((tm, d), bf16),
            pltpu.VMEM((RS + 8, tn), f32),
            pltpu.VMEM((nj, 8, tn), f32),
        ],
        compiler_params=_params("arbitrary", "arbitrary"),
        name="proj_conv",
    )(x, nw.reshape(1, d), w_stack, conv_w)
    return qkv, tails.reshape(m // seq_len, tiles_per_seq, DN_CONV - 1, DN_CONV_DIM)[:, -1]


def _matmul_resid_kernel(a_ref, w_ref, b_ref, r_ref, o_ref):
    o_ref[...] = r_ref[...] + _dot(a_ref[...], w_ref[...]) + b_ref[...]


def matmul_resid(a, w_stack, layer, b, resid):
    m, k = a.shape
    n = w_stack.shape[2]
    tm = min(m, 1024)
    return pl.pallas_call(
        _matmul_resid_kernel,
        grid=(m // tm,),
        in_specs=[
            pl.BlockSpec((tm, k), lambda i: (i, 0)),
            pl.BlockSpec((None, k, n), lambda i: (layer, 0, 0)),
            pl.BlockSpec((1, n), lambda i: (0, 0)),
            pl.BlockSpec((tm, n), lambda i: (i, 0)),
        ],
        out_specs=pl.BlockSpec((tm, n), lambda i: (i, 0)),
        out_shape=jax.ShapeDtypeStruct((m, n), f32),
        compiler_params=_params("parallel"),
        name="matmul_resid",
    )(a, w_stack, b.reshape(1, n), resid)


def _ffn_kernel(final_norm, x_ref, nw_ref, wg_ref, wu_ref, wd_ref, nf_ref, o_ref, xn_ref, acc_ref):
    j = pl.program_id(1)

    @pl.when(j == 0)
    def _():
        xn_ref[...] = _rms(x_ref[...], nw_ref[...]).astype(bf16)
        acc_ref[...] = jnp.zeros_like(acc_ref)

    xn = xn_ref[...]
    tf = wg_ref.shape[1]
    half = (tf // 256) * 128
    for c0, c1 in ((0, half), (half, tf)):
        g = _dot(xn, wg_ref[:, c0:c1])
        u = _dot(xn, wu_ref[:, c0:c1])
        h = (_silu(g) * u).astype(bf16)
        acc_ref[...] += _dot(h, wd_ref[c0:c1, :])

    @pl.when(j == pl.num_programs(1) - 1)
    def _():
        y = x_ref[...] + acc_ref[...]
        if final_norm:
            y = _rms(y, nf_ref[...])
        o_ref[...] = y


def ffn(x, nw, wgu_stack, wd_stack, layer, nf, final_norm):
    m, d = x.shape
    tm = min(m, 1024)
    tf = D_FF // 2
    nj = D_FF // tf
    return pl.pallas_call(
        functools.partial(_ffn_kernel, final_norm),
        grid=(m // tm, nj),
        in_specs=[
            pl.BlockSpec((tm, d), lambda i, j: (i, 0)),
            pl.BlockSpec((1, d), lambda i, j: (0, 0)),
            pl.BlockSpec((None, d, tf), lambda i, j: (layer, 0, j)),
            pl.BlockSpec((None, d, tf), lambda i, j: (layer, 0, j + nj)),
            pl.BlockSpec((None, tf, d), lambda i, j: (layer, j, 0)),
            pl.BlockSpec((1, d), lambda i, j: (0, 0)),
        ],
        out_specs=pl.BlockSpec((tm, d), lambda i, j: (i, 0)),
        out_shape=jax.ShapeDtypeStruct((m, d), f32),
        scratch_shapes=[pltpu.VMEM((tm, d), bf16), pltpu.VMEM((tm, d), f32)],
        compiler_params=_params("parallel", "arbitrary"),
        name="ffn",
    )(x, nw.reshape(1, d), wgu_stack, wgu_stack, wd_stack, nf.reshape(1, d))


def _split3(x):
    hi = x.astype(bf16)
    r1 = x - hi.astype(f32)
    mid = r1.astype(bf16)
    lo = (r1 - mid.astype(f32)).astype(bf16)
    return hi, mid, lo


def _delta_kernel(C, NS, n_iter, has_init, n_alias, pre_conv, *refs):
    if pre_conv:
        qkv_ref, zba_ref = refs[0:2]
        pos = 2
    else:
        proj_ref = refs[0]
        pos = 1
    if has_init:
        conv0_ref, s0_ref = refs[pos:pos + 2]
        pos += 2
    cw_ref, alog_ref, dtb_ref, nw_ref = refs[pos:pos + 4]
    pos += 4 + n_alias
    if pre_conv:
        o_ref, sout_ref = refs[pos:pos + 2]
    else:
        o_ref, convout_ref, sout_ref = refs[pos:pos + 3]
        full_ref, qn_ref, kn_ref = refs[pos + 3:]
    n = pl.program_id(1)
    H = DN_HEAD
    NH = DN_V_HEADS
    NQ = DN_QK_HEADS
    per = NH // NQ
    seqs = range(NS)
    pairs = [(s, h) for s in seqs for h in range(NH)]
    qkp = [(s, qh) for s in seqs for qh in range(NQ)]
    P = range(len(pairs))
    qi = [s * NQ + h // per for s, h in pairs]

    @pl.when(n == 0)
    def _():
        if has_init:
            full_ref[:, 5:8, :] = conv0_ref[...]
            sout_ref[...] = s0_ref[...]
        else:
            if not pre_conv:
                full_ref[:, 5:8, :] = jnp.zeros((NS, DN_CONV - 1, DN_CONV_DIM), f32)
            sout_ref[...] = jnp.zeros_like(sout_ref)

    if pre_conv:
        get_ba = lambda s: zba_ref[s, :, DN_VAL_DIM:DN_VAL_DIM + H]
        get_z = lambda s, h: zba_ref[s, :, h * H:(h + 1) * H]
    else:
        get_ba = lambda s: proj_ref[s, :, DN_CONV_DIM + DN_VAL_DIM:DN_IN_PAD]
        get_z = lambda s, h: proj_ref[s, :, DN_CONV_DIM + h * H:DN_CONV_DIM + (h + 1) * H]
        for s in seqs:
            full_ref[s, 8:8 + C, :] = proj_ref[s, :, 0:DN_CONV_DIM]

    def conv_silu(s, cb):
        cols = slice(cb * H, (cb + 1) * H)
        w = cw_ref[:, cols]
        y = (full_ref[s, 5:5 + C, cols] * w[0:1] + full_ref[s, 6:6 + C, cols] * w[1:2]
             + full_ref[s, 7:7 + C, cols] * w[2:3] + full_ref[s, 8:8 + C, cols] * w[3:4])
        return _silu(y)

    if not pre_conv:
        for s in seqs:
            for cb in range(2 * NQ):
                y = conv_silu(s, cb)
                inv = lax.rsqrt(jnp.sum(y * y, axis=-1, keepdims=True) + L2_EPS)
                if cb < NQ:
                    qn_ref[s * NQ + cb] = y * (inv * (H ** -0.5))
                else:
                    kn_ref[s * NQ + cb - NQ] = y * inv

    ri = lax.broadcasted_iota(jnp.int32, (C, C), 0)
    ci = lax.broadcasted_iota(jnp.int32, (C, C), 1)
    causal = ri >= ci
    strict = ri > ci
    eye = jnp.where(ri == ci, 1.0, 0.0).astype(f32)
    tri16 = jnp.where(causal, 1.0, 0.0).astype(bf16)
    beta, G, GT = [], [], []
    for s in seqs:
        ba = get_ba(s)
        beta.append(_sigmoid(ba[:, 0:NH]))
        xa = ba[:, NH:2 * NH] + dtb_ref[...]
        softplus = jnp.maximum(xa, 0.0) + jnp.log(1.0 + jnp.exp(-jnp.abs(xa)))
        g = -jnp.exp(alog_ref[...]) * softplus
        g3 = jnp.concatenate(_split3(g), axis=1)
        G3 = _dot(tri16, g3)
        G.append(G3[:, 0:NH] + G3[:, NH:2 * NH] + G3[:, 2 * NH:3 * NH])
        GT3 = lax.dot_general(g3, tri16, (((0,), (1,)), ((), ())), preferred_element_type=f32)
        GT.append(GT3[0:NH] + GT3[NH:2 * NH] + GT3[2 * NH:3 * NH])

    if pre_conv:
        qs = [qkv_ref[s, :, qh * H:(qh + 1) * H] for s, qh in qkp]
        ks = [qkv_ref[s, :, DN_KEY_DIM + qh * H:DN_KEY_DIM + (qh + 1) * H] for s, qh in qkp]
    else:
        ks = [kn_ref[i] for i in range(len(qkp))]
        qs = [qn_ref[i] for i in range(len(qkp))]
    kqs = [_dot_nt(jnp.concatenate([ks[i], qs[i]], axis=0).astype(bf16), ks[i].astype(bf16))
           for i in range(len(qkp))]
    kTs = [ks[i].T for i in range(len(qkp))]
    Gc = [G[s][:, h:h + 1] for s, h in pairs]
    Gr = [GT[s][h:h + 1, :] for s, h in pairs]
    bc = [beta[s][:, h:h + 1] for s, h in pairs]
    decay = [jnp.where(causal, jnp.exp(jnp.where(causal, Gc[p] - Gr[p], 0.0)), 0.0) for p in P]
    M = [jnp.where(strict, -(bc[p] * kqs[qi[p]][:C] * decay[p]), 0.0) for p in P]
    T = [eye + M[p] for p in P]
    Mp = [_dot(M[p].astype(bf16), M[p].astype(bf16)) for p in P]
    for it in range(n_iter):
        if it < n_iter - 1:
            tm = [_dot(jnp.concatenate([T[p], Mp[p]], axis=0).astype(bf16), Mp[p].astype(bf16)) for p in P]
            T = [T[p] + tm[p][:C] for p in P]
            Mp = [tm[p][C:] for p in P]
        else:
            tm = [_dot(T[p].astype(bf16), Mp[p].astype(bf16)) for p in P]
            T = [T[p] + tm[p] for p in P]
    if pre_conv:
        vs = [qkv_ref[s, :, 2 * DN_KEY_DIM + h * H:2 * DN_KEY_DIM + (h + 1) * H] for s, h in pairs]
    else:
        vs = [conv_silu(s, 2 * NQ + h) for s, h in pairs]
        for s in seqs:
            tail = full_ref[s, C + 5:C + 8, :]
            full_ref[s, 5:8, :] = tail
            convout_ref[s] = tail

    eG = [jnp.exp(Gc[p]) for p in P]
    uw = [_dot(T[p].astype(bf16),
               jnp.concatenate([vs[p] * bc[p], ks[qi[p]] * (bc[p] * eG[p])], axis=1).astype(bf16))
          for p in P]
    S = [sout_ref[s, h] for s, h in pairs]
    ws = [_dot(jnp.concatenate([uw[p][:, H:], qs[qi[p]] * eG[p]], axis=0).astype(bf16), S[p].astype(bf16))
          for p in P]
    unew = [uw[p][:, :H] - ws[p][:C] for p in P]
    Gl = [Gc[p][C - 1:C, :] for p in P]
    r = [_dot(jnp.concatenate([kqs[qi[p]][C:] * decay[p], kTs[qi[p]] * jnp.exp(Gl[p] - Gr[p])],
                              axis=0).astype(bf16), unew[p].astype(bf16))
         for p in P]
    for p, (s, h) in enumerate(pairs):
        sout_ref[s, h] = S[p] * jnp.exp(Gl[p]) + r[p][C:]
        on = _rms(ws[p][C:] + r[p][:C], nw_ref[...])
        o_ref[s, :, h * H:(h + 1) * H] = (on * _silu(get_z(s, h))).astype(bf16)


def delta_mixer(proj, init, prev_out, layer, n_layers, conv_w, a_log, dt_bias, norm_w, C, NS):
    pre_conv = isinstance(proj, tuple)
    assert not (pre_conv and init is not None)
    B, L, _ = proj[0].shape if pre_conv else proj.shape
    n_iter = max(1, (C - 1).bit_length()) - 1
    H = DN_HEAD
    has_init = init is not None
    n_alias = 0 if prev_out is None else len(prev_out)
    conv_spec = pl.BlockSpec((None, NS, DN_CONV - 1, DN_CONV_DIM), lambda b, n: (layer, b, 0, 0))
    state_spec = pl.BlockSpec((None, NS, DN_V_HEADS, H, H), lambda b, n: (layer, b, 0, 0, 0))
    small = lambda shape: pl.BlockSpec(shape, lambda b, n: (0,) * len(shape))
    if pre_conv:
        in_specs = [pl.BlockSpec((NS, C, DN_CONV_DIM), lambda b, n: (b, n, 0)),
                    pl.BlockSpec((NS, C, DN_ZBA_PAD), lambda b, n: (b, n, 0))]
        args = list(proj)
    else:
        in_specs = [pl.BlockSpec((NS, C, DN_IN_PAD), lambda b, n: (b, n, 0))]
        args = [proj]
    if has_init:
        in_specs += [conv_spec, state_spec]
        args += list(init)
    in_specs += [small((DN_CONV, DN_CONV_DIM)), small((1, DN_V_HEADS)), small((1, DN_V_HEADS)), small((1, H))]
    args += [conv_w, a_log.reshape(1, -1), dt_bias.reshape(1, -1), norm_w.reshape(1, -1)]
    out_specs = [pl.BlockSpec((NS, C, DN_VAL_DIM), lambda b, n: (b, n, 0)), conv_spec, state_spec]
    out_shape = [
        jax.ShapeDtypeStruct((B, L, DN_VAL_DIM), bf16),
        jax.ShapeDtypeStruct((n_layers, B, DN_CONV - 1, DN_CONV_DIM), f32),
        jax.ShapeDtypeStruct((n_layers, B, DN_V_HEADS, H, H), f32),
    ]
    scratch_shapes = [
        pltpu.VMEM((NS, C + 8, DN_CONV_DIM), f32),
        pltpu.VMEM((NS * DN_QK_HEADS, C, H), f32),
        pltpu.VMEM((NS * DN_QK_HEADS, C, H), f32),
    ]
    if pre_conv:
        del out_specs[1], out_shape[1]
        scratch_shapes = []
    aliases = {}
    if prev_out is not None:
        first_out = len(out_specs) - len(prev_out)
        aliases = {len(args) + k: first_out + k for k in range(len(prev_out))}
        in_specs += [pl.BlockSpec(memory_space=pl.ANY)] * len(prev_out)
        args += list(prev_out)
    return pl.pallas_call(
        functools.partial(_delta_kernel, C, NS, n_iter, has_init, n_alias, pre_conv),
        grid=(B // NS, L // C),
        in_specs=in_specs,
        out_specs=out_specs,
        out_shape=out_shape,
        scratch_shapes=scratch_shapes,
        input_output_aliases=aliases,
        compiler_params=_params("parallel", "arbitrary"),
        name="delta_mixer",
    )(*args)


def _swa_bias_table():
    W = WINDOW
    ri = lax.broadcasted_iota(jnp.int32, (W, 2 * W), 0)
    ci = lax.broadcasted_iota(jnp.int32, (W, 2 * W), 1)
    rel = W + ri - ci
    band = (rel >= 0) & (rel <= WINDOW)
    slopes = jnp.asarray(ALIBI_SLOPES, f32).reshape(SWA_HEADS, 1, 1)
    alibi = -slopes * rel.astype(f32)[None]
    later = jnp.where(band[None], alibi, -jnp.inf)
    first = jnp.where((band & (ci >= W))[None], alibi, -jnp.inf)
    return jnp.stack([first, later])


def _swa_prompt_kernel(NB, sinks_ref, bias_ref, q_ref, kc_ref, kp_ref, vc_ref, vp_ref, o_ref):
    hd = SWA_HEAD_DIM
    k16, v16 = [], []
    for b in range(NB):
        for kv in range(SWA_KV_HEADS):
            cs = slice(kv * hd, (kv + 1) * hd)
            k16.append(jnp.concatenate([kp_ref[b, :, cs], kc_ref[b, :, cs]], axis=0).astype(bf16))
            v16.append(jnp.concatenate([vp_ref[b, :, cs], vc_ref[b, :, cs]], axis=0).astype(bf16))
    all_pairs = [(b, h) for b in range(NB) for h in range(SWA_HEADS)]
    for g0 in range(0, len(all_pairs), SWA_HEAD_GROUP):
        pairs = all_pairs[g0:g0 + SWA_HEAD_GROUP]
        kvi = [b * SWA_KV_HEADS + h // SWA_GROUPS for b, h in pairs]
        s = [_dot_nt((q_ref[b, :, h * hd:(h + 1) * hd] * SWA_SCALE).astype(bf16), k16[kvi[i]])
             for i, (b, h) in enumerate(pairs)]
        p = []
        for i, (b, h) in enumerate(pairs):
            sh = s[i] + bias_ref[0, h]
            sk = sinks_ref[h]
            m = jnp.maximum(jnp.max(sh, axis=-1, keepdims=True), sk)
            e = jnp.exp(sh - m)
            denom = jnp.sum(e, axis=-1, keepdims=True) + jnp.exp(sk - m)
            p.append((e / denom).astype(bf16))
        o = [_dot(p[i], v16[kvi[i]]) for i in range(len(pairs))]
        for i, (b, h) in enumerate(pairs):
            o_ref[b, :, h * hd:(h + 1) * hd] = o[i].astype(bf16)


def swa_prompt_attention(qkv, sinks):
    B, L, _ = qkv.shape
    W = WINDOW
    kb = SWA_Q_DIM // SWA_KV_DIM
    prev = lambda n: jnp.maximum(n - 1, 0)
    NB = 1
    return pl.pallas_call(
        functools.partial(_swa_prompt_kernel, NB),
        grid=(B // NB, L // W),
        in_specs=[
            pl.BlockSpec(memory_space=pltpu.SMEM),
            pl.BlockSpec((1, SWA_HEADS, W, 2 * W), lambda b, n: (jnp.minimum(n, 1), 0, 0, 0)),
            pl.BlockSpec((NB, W, SWA_Q_DIM), lambda b, n: (b, n, 0)),
            pl.BlockSpec((NB, W, SWA_KV_DIM), lambda b, n: (b, n, kb)),
            pl.BlockSpec((NB, W, SWA_KV_DIM), lambda b, n: (b, prev(n), kb)),
            pl.BlockSpec((NB, W, SWA_KV_DIM), lambda b, n: (b, n, kb + 1)),
            pl.BlockSpec((NB, W, SWA_KV_DIM), lambda b, n: (b, prev(n), kb + 1)),
        ],
        out_specs=pl.BlockSpec((NB, W, SWA_Q_DIM), lambda b, n: (b, n, 0)),
        out_shape=jax.ShapeDtypeStruct((B, L, SWA_Q_DIM), bf16),
        compiler_params=_params("parallel", "arbitrary"),
        name="swa_prompt",
    )(sinks, _swa_bias_table(), qkv, qkv, qkv, qkv, qkv)


def _swa_sample_kernel(NS, L, R, sinks_ref, qkv_ref, ck_ref, cv_ref, o_ref):
    hd = SWA_HEAD_DIM
    G = SWA_GROUPS
    rows = G * L
    t_c = lax.broadcasted_iota(jnp.int32, (rows, R), 0) % L
    j_c = lax.broadcasted_iota(jnp.int32, (rows, R), 1)
    rel_c = R + t_c - j_c
    valid_c = (rel_c >= 0) & (rel_c <= WINDOW)
    t_n = lax.broadcasted_iota(jnp.int32, (rows, L), 0) % L
    j_n = lax.broadcasted_iota(jnp.int32, (rows, L), 1)
    rel_n = t_n - j_n
    valid_n = (rel_n >= 0) & (rel_n <= WINDOW)
    g_row = lax.broadcasted_iota(jnp.int32, (rows, 1), 0) // L
    bias_c, bias_n, sks = [], [], []
    for kv in range(SWA_KV_HEADS):
        slope = jnp.zeros((rows, 1), f32)
        sk = jnp.zeros((rows, 1), f32)
        for gi in range(G):
            h = kv * G + gi
            slope = jnp.where(g_row == gi, ALIBI_SLOPES[h], slope)
            sk = jnp.where(g_row == gi, sinks_ref[h], sk)
        bias_c.append(slope * rel_c.astype(f32))
        bias_n.append(slope * rel_n.astype(f32))
        sks.append(sk)
    pairs = [(b, kv) for b in range(NS) for kv in range(SWA_KV_HEADS)]
    kcol = lambda kv: slice(SWA_Q_DIM + kv * hd, SWA_Q_DIM + (kv + 1) * hd)
    vcol = lambda kv: slice(SWA_Q_DIM + SWA_KV_DIM + kv * hd, SWA_Q_DIM + SWA_KV_DIM + (kv + 1) * hd)
    q16 = [jnp.concatenate([qkv_ref[b, :, (kv * G + gi) * hd:(kv * G + gi + 1) * hd] for gi in range(G)],
                           axis=0).astype(bf16) for b, kv in pairs]
    s_c = [_dot_nt(q16[i], ck_ref[b, :, kv * hd:(kv + 1) * hd].astype(bf16)) for i, (b, kv) in enumerate(pairs)]
    s_n = [_dot_nt(q16[i], qkv_ref[b, :, kcol(kv)].astype(bf16)) for i, (b, kv) in enumerate(pairs)]
    p_c, p_n = [], []
    for i, (b, kv) in enumerate(pairs):
        sc = jnp.where(valid_c, s_c[i] * SWA_SCALE - bias_c[kv], -jnp.inf)
        sn = jnp.where(valid_n, s_n[i] * SWA_SCALE - bias_n[kv], -jnp.inf)
        m = jnp.maximum(jnp.maximum(jnp.max(sc, axis=-1, keepdims=True),
                                    jnp.max(sn, axis=-1, keepdims=True)), sks[kv])
        ec = jnp.exp(sc - m)
        en = jnp.exp(sn - m)
        denom = jnp.sum(ec, axis=-1, keepdims=True) + jnp.sum(en, axis=-1, keepdims=True) + jnp.exp(sks[kv] - m)
        p_c.append((ec / denom).astype(bf16))
        p_n.append((en / denom).astype(bf16))
    o = [_dot(p_c[i], cv_ref[b, :, kv * hd:(kv + 1) * hd].astype(bf16))
         + _dot(p_n[i], qkv_ref[b, :, vcol(kv)].astype(bf16)) for i, (b, kv) in enumerate(pairs)]
    for i, (b, kv) in enumerate(pairs):
        for gi in range(G):
            h = kv * G + gi
            o_ref[b, :, h * hd:(h + 1) * hd] = o[i][gi * L:(gi + 1) * L].astype(bf16)


def swa_sample_attention(qkv, cache_k, cache_v, sinks):
    B, L, _ = qkv.shape
    R = cache_k.shape[1]
    NS = min(B, SWA_SAMPLE_SEQS)
    return pl.pallas_call(
        functools.partial(_swa_sample_kernel, NS, L, R),
        grid=(B // NS,),
        in_specs=[
            pl.BlockSpec(memory_space=pltpu.SMEM),
            pl.BlockSpec((NS, L, SWA_QKV_DIM), lambda b: (b, 0, 0)),
            pl.BlockSpec((NS, R, SWA_KV_DIM), lambda b: (b, 0, 0)),
            pl.BlockSpec((NS, R, SWA_KV_DIM), lambda b: (b, 0, 0)),
        ],
        out_specs=pl.BlockSpec((NS, L, SWA_Q_DIM), lambda b: (b, 0, 0)),
        out_shape=jax.ShapeDtypeStruct((B, L, SWA_Q_DIM), bf16),
        compiler_params=_params("parallel"),
        name="swa_sample",
    )(sinks, qkv, cache_k, cache_v)


def kernel(x_prompt, x_sample, state_conv, state_delta, cache_k, cache_v, norm_mix, norm_ffn, norm_final,
           dn_w_in, dn_conv_w, dn_a_log, dn_dt_bias, dn_norm_w, dn_w_out, swa_w_qkv, swa_b_qkv, swa_sinks,
           swa_w_o, swa_b_o, ffn_w_gu, ffn_w_down):
    Bp, Lp, D = x_prompt.shape
    Bs, Ls, _ = x_sample.shape
    depth = norm_mix.shape[0]
    n_delta = dn_w_in.shape[0]
    yp = x_prompt.reshape(Bp * Lp, D)
    ys = x_sample.reshape(Bs * Ls, D)
    R = cache_k.shape[2]

    w_in = jnp.pad(dn_w_in.astype(bf16), ((0, 0), (0, 0), (0, DN_IN_PAD - DN_IN_DIM)))
    w_zba = w_in[:, :, DN_CONV_DIM:]
    w_out = dn_w_out.astype(bf16)
    w_qkv = swa_w_qkv.astype(bf16)
    w_o = swa_w_o.astype(bf16)
    w_gu = ffn_w_gu.astype(bf16)
    w_down = ffn_w_down.astype(bf16)
    zero_b = jnp.zeros((D,), f32)
    zero_in = jnp.zeros((DN_IN_PAD,), f32)
    zero_zba = jnp.zeros((DN_ZBA_PAD,), f32)

    kp_list, vp_list, ks_list, vs_list = [], [], [], []
    prompt_state = None
    sample_states = None
    conv_p = []
    for i in range(depth):
        j = i // 2
        if i % 2 == 0:
            wts = (dn_conv_w[j], dn_a_log[j], dn_dt_bias[j], dn_norm_w[j])
            qkv, cb = proj_conv(yp, norm_mix[i], w_in, j, dn_conv_w[j], Lp)
            zba = rms_matmul(yp, norm_mix[i], w_zba, j, zero_zba, DN_ZBA_PAD)
            o, sn = delta_mixer((qkv.reshape(Bp, Lp, DN_CONV_DIM), zba.reshape(Bp, Lp, DN_ZBA_PAD)), None,
                                prompt_state, j, n_delta, *wts, C=min(DN_CHUNK, Lp), NS=DN_PROMPT_SEQS)
            prompt_state = (sn,)
            conv_p.append(cb)
            yp = matmul_resid(o.reshape(Bp * Lp, DN_VAL_DIM), w_out, j, zero_b, yp)
            proj = rms_matmul(ys, norm_mix[i], w_in, j, zero_in, 896).reshape(Bs, Ls, DN_IN_PAD)
            o, cb, sn = delta_mixer(proj, (state_conv, state_delta), sample_states, j, n_delta, *wts, C=Ls,
                                    NS=min(Bs, DN_SAMPLE_SEQS))
            sample_states = (cb, sn)
            ys = matmul_resid(o.reshape(Bs * Ls, DN_VAL_DIM), w_out, j, zero_b, ys)
        else:
            qkv = rms_matmul(yp, norm_mix[i], w_qkv, j, swa_b_qkv[j], 768).reshape(Bp, Lp, SWA_QKV_DIM)
            o = swa_prompt_attention(qkv, swa_sinks[j])
            yp = matmul_resid(o.reshape(Bp * Lp, SWA_Q_DIM), w_o, j, swa_b_o[j], yp)
            Rp = min(WINDOW, Lp)
            kp_list.append(qkv[:, Lp - Rp:, SWA_Q_DIM:SWA_Q_DIM + SWA_KV_DIM].reshape(Bp, Rp, SWA_KV_HEADS, SWA_HEAD_DIM))
            vp_list.append(qkv[:, Lp - Rp:, SWA_Q_DIM + SWA_KV_DIM:].reshape(Bp, Rp, SWA_KV_HEADS, SWA_HEAD_DIM))
            qkv = rms_matmul(ys, norm_mix[i], w_qkv, j, swa_b_qkv[j], 768).reshape(Bs, Ls, SWA_QKV_DIM)
            o = swa_sample_attention(qkv, cache_k[j].reshape(Bs, R, SWA_KV_DIM), cache_v[j].reshape(Bs, R, SWA_KV_DIM),
                                     swa_sinks[j])
            ys = matmul_resid(o.reshape(Bs * Ls, SWA_Q_DIM), w_o, j, swa_b_o[j], ys)
            ks_list.append(qkv[:, :, SWA_Q_DIM:SWA_Q_DIM + SWA_KV_DIM].reshape(Bs, Ls, SWA_KV_HEADS, SWA_HEAD_DIM))
            vs_list.append(qkv[:, :, SWA_Q_DIM + SWA_KV_DIM:].reshape(Bs, Ls, SWA_KV_HEADS, SWA_HEAD_DIM))
        last = i == depth - 1
        yp = ffn(yp, norm_ffn[i], w_gu, w_down, i, norm_final, last)
        ys = ffn(ys, norm_ffn[i], w_gu, w_down, i, norm_final, last)

    return (yp.reshape(Bp, Lp, D), ys.reshape(Bs, Ls, D),
            jnp.stack(conv_p), prompt_state[0], jnp.stack(kp_list), jnp.stack(vp_list),
            sample_states[0], sample_states[1], jnp.stack(ks_list), jnp.stack(vs_list))
```
